```python
import jax, jax.numpy as jnp
from jax import lax
import numpy as np

D_MODEL = 1024
BATCH = 8
SEQ = 4096
DEPTH = 2

CHUNK = 64
CONV_WIDTH = 512
CONV_KERNEL = 31
POOL_WINDOWS = (2, 4, 8, 16)
POOL_GROUP = 128
POOL_WIDTH = POOL_GROUP * len(POOL_WINDOWS)
SGU_HEADS = 4
SGU_HEAD_DIM = 128
SGU_WIDTH = SGU_HEADS * SGU_HEAD_DIM
SGU_BLOCK = 128
N_BRANCH = 3
IN_WIDTH = 2 * CONV_WIDTH + POOL_WIDTH + 2 * SGU_WIDTH
D_FF = 2816
N_EXPERTS = 8
TOP_K = 2
D_FF_EXPERT = 3584
N_DENSE = (DEPTH + 1) // 2
N_MOE = DEPTH // 2
EPS = 1e-6

kernel_name = "hybrid_gated_conv_pool_sgu_moe_trunk"


def rmsnorm(x, g):
    xf = x.astype(jnp.float32)
    y = xf * lax.rsqrt(jnp.mean(xf * xf, axis=-1, keepdims=True) + EPS)
    return (y * g.astype(jnp.float32)).astype(x.dtype)


def layernorm(x, g, b):
    xf = x.astype(jnp.float32)
    mu = jnp.mean(xf, axis=-1, keepdims=True)
    var = jnp.mean(jnp.square(xf - mu), axis=-1, keepdims=True)
    y = (xf - mu) * lax.rsqrt(var + EPS)
    return (y * g.astype(jnp.float32) + b.astype(jnp.float32)).astype(x.dtype)


def conv_branch(z, w_dw, b_dw, ln_g, ln_b, w_out):
    a, gate = jnp.split(z, 2, axis=-1)
    h = a * jax.nn.sigmoid(gate)
    h = lax.conv_general_dilated(
        h, w_dw[:, None, :].astype(h.dtype), window_strides=(1,),
        padding=((CONV_KERNEL - 1, 0),),
        dimension_numbers=("NWC", "WIO", "NWC"),
        feature_group_count=CONV_WIDTH) + b_dw
    h = jax.nn.silu(layernorm(h, ln_g, ln_b))
    return h @ w_out


def pool_branch(p, w_mix, scale, w_out):
    B, S, _ = p.shape
    pf = p.astype(jnp.float32)
    csum = jnp.cumsum(pf, axis=1)
    t = jnp.arange(S)
    outs = []
    for g, w in enumerate(POOL_WINDOWS):
        sl = slice(g * POOL_GROUP, (g + 1) * POOL_GROUP)
        cg = csum[..., sl]
        lagged = jnp.pad(cg, ((0, 0), (w, 0), (0, 0)))[:, :S]
        cnt = jnp.minimum(t + 1, w).astype(jnp.float32)[None, :, None]
        outs.append((cg - lagged) / cnt - pf[..., sl])
    d = jnp.stack(outs, axis=2).astype(p.dtype)
    y = jnp.einsum("bsgc,gcd->bsgd", d, w_mix).reshape(B, S, POOL_WIDTH) * scale
    return y @ w_out


def sgu_branch(z, ln_g, ln_b, w_s, b_s, w_out):
    B, S, _ = z.shape
    z = jax.nn.gelu(z)
    u, v = jnp.split(z, 2, axis=-1)
    v = layernorm(v, ln_g, ln_b).reshape(B, S // SGU_BLOCK, SGU_BLOCK, SGU_HEADS, SGU_HEAD_DIM)
    mask = jnp.tril(jnp.ones((SGU_BLOCK, SGU_BLOCK), dtype=bool))
    ws = jnp.where(mask[None], w_s, jnp.zeros_like(w_s))
    sv = jnp.einsum("hqp,bnphd->bnqhd", ws, v) + b_s.T[None, None, :, :, None]
    y = u * sv.reshape(B, S, SGU_WIDTH)
    return y @ w_out


def dense_swiglu(h, w_up, w_down):
    a, b = jnp.split(h @ w_up, 2, axis=-1)
    return (jax.nn.silu(a) * b) @ w_down


def moe_swiglu(h, w_router, w_up, w_down):
    B, S, D = h.shape
    hf = h.reshape(B * S, D)
    logits = (hf @ w_router).astype(jnp.float32)
    top_v, top_i = lax.top_k(logits, TOP_K)
    wts = jax.nn.softmax(top_v, axis=-1)
    gate = jnp.sum(jax.nn.one_hot(top_i, N_EXPERTS, dtype=jnp.float32) * wts[..., None], axis=1)
    out = jnp.zeros((B * S, D), jnp.float32)
    for e in range(N_EXPERTS):
        a, b = jnp.split(hf @ w_up[e], 2, axis=-1)
        out = out + gate[:, e:e + 1] * ((jax.nn.silu(a) * b) @ w_down[e]).astype(jnp.float32)
    return out.astype(h.dtype).reshape(B, S, D)


def setup_inputs(seed: int = 0) -> dict:
    key = jax.random.key(seed)
    ks = jax.random.split(key, 32)
    f = jnp.float32
    D = D_MODEL

    def nrm(k, shape, fan_in):
        return jax.random.normal(k, shape, f) * (fan_in ** -0.5)

    def gain(k, shape):
        return 1.0 + 0.05 * jax.random.normal(k, shape, f)

    def bias(k, shape):
        return 0.02 * jax.random.normal(k, shape, f)

    return {
        "x": jax.random.normal(ks[0], (BATCH, SEQ, D), f),
        "norm_mix": gain(ks[1], (DEPTH, D)),
        "w_in": nrm(ks[2], (DEPTH, D, IN_WIDTH), D),
        "b_in": bias(ks[3], (DEPTH, IN_WIDTH)),
        "conv_w": nrm(ks[4], (DEPTH, CONV_KERNEL, CONV_WIDTH), CONV_KERNEL),
        "conv_b": bias(ks[5], (DEPTH, CONV_WIDTH)),
        "conv_ln_g": gain(ks[6], (DEPTH, CONV_WIDTH)),
        "conv_ln_b": bias(ks[7], (DEPTH, CONV_WIDTH)),
        "conv_out": nrm(ks[8], (DEPTH, CONV_WIDTH, D), CONV_WIDTH),
        "pool_mix": nrm(ks[9], (DEPTH, len(POOL_WINDOWS), POOL_GROUP, POOL_GROUP), POOL_GROUP),
        "pool_scale": gain(ks[10], (DEPTH, POOL_WIDTH)),
        "pool_out": nrm(ks[11], (DEPTH, POOL_WIDTH, D), POOL_WIDTH),
        "sgu_ln_g": gain(ks[12], (DEPTH, SGU_WIDTH)),
        "sgu_ln_b": bias(ks[13], (DEPTH, SGU_WIDTH)),
        "sgu_w": nrm(ks[14], (DEPTH, SGU_HEADS, SGU_BLOCK, SGU_BLOCK), SGU_BLOCK),
        "sgu_b": gain(ks[15], (DEPTH, SGU_HEADS, SGU_BLOCK)),
        "sgu_out": nrm(ks[16], (DEPTH, SGU_WIDTH, D), SGU_WIDTH),
        "w_gate": nrm(ks[17], (DEPTH, D, N_BRANCH * D), D),
        "b_gate": bias(ks[18], (DEPTH, N_BRANCH * D)),
        "w_o": nrm(ks[19], (DEPTH, D, D), D),
        "norm_ffn": gain(ks[20], (DEPTH, D)),
        "ffn_w_up": nrm(ks[21], (N_DENSE, D, 2 * D_FF), D),
        "ffn_w_down": nrm(ks[22], (N_DENSE, D_FF, D), D_FF),
        "moe_router": nrm(ks[23], (N_MOE, D, N_EXPERTS), D),
        "moe_w_up": nrm(ks[24], (N_MOE, N_EXPERTS, D, 2 * D_FF_EXPERT), D),
        "moe_w_down": nrm(ks[25], (N_MOE, N_EXPERTS, D_FF_EXPERT, D), D_FF_EXPERT),
        "norm_final": gain(ks[26], (D,)),
    }


def reference(x, norm_mix, w_in, b_in, conv_w, conv_b, conv_ln_g, conv_ln_b, conv_out,
              pool_mix, pool_scale, pool_out, sgu_ln_g, sgu_ln_b, sgu_w, sgu_b, sgu_out,
              w_gate, b_gate, w_o, norm_ffn, ffn_w_up, ffn_w_down,
              moe_router, moe_w_up, moe_w_down, norm_final):
    B, S, D = x.shape
    c1 = 2 * CONV_WIDTH
    c2 = c1 + POOL_WIDTH
    for i in range(DEPTH):
        h = rmsnorm(x, norm_mix[i])
        z = h @ w_in[i] + b_in[i]
        z_conv, z_pool, z_sgu = z[..., :c1], z[..., c1:c2], z[..., c2:]
        y_a = conv_branch(z_conv, conv_w[i], conv_b[i], conv_ln_g[i], conv_ln_b[i], conv_out[i])
        y_b = pool_branch(z_pool, pool_mix[i], pool_scale[i], pool_out[i])
        y_c = sgu_branch(z_sgu, sgu_ln_g[i], sgu_ln_b[i], sgu_w[i], sgu_b[i], sgu_out[i])
        g = jax.nn.sigmoid(h @ w_gate[i] + b_gate[i]).reshape(B, S, N_BRANCH, D)
        merged = g[:, :, 0] * y_a + g[:, :, 1] * y_b + g[:, :, 2] * y_c
        x = x + merged @ w_o[i]
        h = rmsnorm(x, norm_ffn[i])
        if i % 2 == 0:
            x = x + dense_swiglu(h, ffn_w_up[i // 2], ffn_w_down[i // 2])
        else:
            x = x + moe_swiglu(h, moe_router[i // 2], moe_w_up[i // 2], moe_w_down[i // 2])
    return rmsnorm(x, norm_final)
```

```python
import functools

import jax
import jax.numpy as jnp
from jax import lax
from jax.experimental import pallas as pl
from jax.experimental.pallas import tpu as pltpu

D_MODEL = 1024
DEPTH = 2
CONV_WIDTH = 512
CONV_KERNEL = 31
POOL_WINDOWS = (2, 4, 8, 16)
POOL_GROUP = 128
POOL_WIDTH = POOL_GROUP * len(POOL_WINDOWS)
SGU_HEADS = 4
SGU_HEAD_DIM = 128
SGU_WIDTH = SGU_HEADS * SGU_HEAD_DIM
SGU_BLOCK = 128
N_BRANCH = 3
IN_WIDTH = 2 * CONV_WIDTH + POOL_WIDTH + 2 * SGU_WIDTH
D_FF = 2816
N_EXPERTS = 8
TOP_K = 2
D_FF_EXPERT = 3584
EPS = 1e-6

LANES = 128
CONV_HALO = 32
POOL_HALO = 16
VMEM_LIMIT = 56 * 1024 * 1024

TM_MIX = 512
CONV_ROWS = 64
TM_FFN = 512
TM_MOE = 1024
TF_MOE = 512

F32 = jnp.float32
BF16 = jnp.bfloat16


def _const_spec(shape):
    zeros = (0,) * len(shape)
    return pl.BlockSpec(shape, lambda *_: zeros, pipeline_mode=pl.Buffered(1))


def _rmsnorm(x, g):
    return x * lax.rsqrt(jnp.mean(x * x, axis=-1, keepdims=True) + EPS) * g


def _layernorm(x, g, b):
    mu = jnp.mean(x, axis=-1, keepdims=True)
    xc = x - mu
    var = jnp.mean(xc * xc, axis=-1, keepdims=True)
    return xc * lax.rsqrt(var + EPS) * g + b


def _dot(a, b):
    return jnp.dot(a, b, preferred_element_type=F32)


def _mixer_kernel(x_ref, g_ref, w_in_ref, b_in_ref, cw_ref, cb_ref, clg_ref, clb_ref, cout_ref,
                  pmix_ref, pscale_ref, pout_ref, slg_ref, slb_ref, sw_ref, sb_ref, sout_ref,
                  wg_ref, bg_ref, wo_ref, o_ref, conv_ext, pool_ext):
    tm = x_ref.shape[1]
    j = pl.program_id(1)
    x = x_ref[0]
    hb = _rmsnorm(x, g_ref[...]).astype(BF16)

    def proj(lo, hi):
        return _dot(hb, w_in_ref[:, lo:hi]) + b_in_ref[:, lo:hi]

    def gate(k):
        lo, hi = k * D_MODEL, (k + 1) * D_MODEL
        return jax.nn.sigmoid(_dot(hb, wg_ref[:, lo:hi]) + bg_ref[:, lo:hi])

    @pl.when(j == 0)
    def _():
        conv_ext[0:CONV_HALO, :] = jnp.zeros((CONV_HALO, CONV_WIDTH), F32)
        pool_ext[0:POOL_HALO, :] = jnp.zeros((POOL_HALO, POOL_WIDTH), F32)

    @pl.when(j > 0)
    def _():
        conv_ext[0:CONV_HALO, :] = conv_ext[tm:tm + CONV_HALO, :]
        pool_ext[0:POOL_HALO, :] = pool_ext[tm:tm + POOL_HALO, :]

    c0 = 0
    conv_ext[CONV_HALO:, :] = proj(c0, c0 + CONV_WIDTH) * jax.nn.sigmoid(
        proj(c0 + CONV_WIDTH, c0 + 2 * CONV_WIDTH))
    base = CONV_HALO - (CONV_KERNEL - 1)
    chunks = []
    for r in range(0, tm, CONV_ROWS):
        acc = jnp.zeros((CONV_ROWS, CONV_WIDTH), F32) + cb_ref[...]
        for k in range(CONV_KERNEL):
            acc = acc + conv_ext[r + base + k:r + base + k + CONV_ROWS, :] * cw_ref[k:k + 1, :]
        hn = _layernorm(acc, clg_ref[...], clb_ref[...])
        chunks.append((hn * jax.nn.sigmoid(hn)).astype(BF16))
    y_a = _dot(jnp.concatenate(chunks, axis=0), cout_ref[...])
    merged = gate(0) * y_a

    c1 = 2 * CONV_WIDTH
    p = proj(c1, c1 + POOL_WIDTH)
    pool_ext[POOL_HALO:, :] = p
    pos = j * tm + lax.broadcasted_iota(jnp.int32, (tm, 1), 0)
    mixed = []
    for gi, w in enumerate(POOL_WINDOWS):
        lo, hi = gi * POOL_GROUP, (gi + 1) * POOL_GROUP
        s = pool_ext[POOL_HALO:POOL_HALO + tm, lo:hi]
        for i in range(1, w):
            s = s + pool_ext[POOL_HALO - i:POOL_HALO - i + tm, lo:hi]
        cnt = jnp.minimum(pos + 1, w).astype(F32)
        d = s / cnt - p[:, lo:hi]
        mixed.append(_dot(d.astype(BF16), pmix_ref[gi]))
    yb = jnp.concatenate(mixed, axis=-1) * pscale_ref[...]
    merged = merged + gate(1) * _dot(yb.astype(BF16), pout_ref[...])

    c2 = c1 + POOL_WIDTH
    u = jax.nn.gelu(proj(c2, c2 + SGU_WIDTH), approximate=True)
    v = jax.nn.gelu(proj(c2 + SGU_WIDTH, c2 + 2 * SGU_WIDTH), approximate=True)
    vn = _layernorm(v, slg_ref[...], slb_ref[...]).astype(BF16)
    row = lax.broadcasted_iota(jnp.int32, (SGU_BLOCK, SGU_BLOCK), 0)
    col = lax.broadcasted_iota(jnp.int32, (SGU_BLOCK, SGU_BLOCK), 1)
    ws = [jnp.where(row >= col, sw_ref[hd], 0.0).astype(BF16) for hd in range(SGU_HEADS)]
    blocks = []
    for r in range(0, tm, SGU_BLOCK):
        heads = [_dot(ws[hd], vn[r:r + SGU_BLOCK, hd * SGU_HEAD_DIM:(hd + 1) * SGU_HEAD_DIM])
                 for hd in range(SGU_HEADS)]
        blocks.append(jnp.concatenate(heads, axis=-1) + sb_ref[...])
    y = u * jnp.concatenate(blocks, axis=0)
    merged = merged + gate(2) * _dot(y.astype(BF16), sout_ref[...])

    o_ref[0] = x + _dot(merged.astype(BF16), wo_ref[...])


def _mixer_layer(x, p):
    B, S, D = x.shape
    tm = TM_MIX
    consts = [p["norm"], p["w_in"], p["b_in"], p["conv_w"], p["conv_b"], p["conv_ln_g"], p["conv_ln_b"],
              p["conv_out"], p["pool_mix"], p["pool_scale"], p["pool_out"], p["sgu_ln_g"], p["sgu_ln_b"],
              p["sgu_w"], p["sgu_b"], p["sgu_out"], p["w_gate"], p["b_gate"], p["w_o"]]
    x_spec = pl.BlockSpec((1, tm, D), lambda b, j: (b, j, 0))
    return pl.pallas_call(
        _mixer_kernel,
        grid=(B, S // tm),
        in_specs=[x_spec] + [_const_spec(c.shape) for c in consts],
        out_specs=x_spec,
        out_shape=jax.ShapeDtypeStruct(x.shape, F32),
        scratch_shapes=[pltpu.VMEM((tm + CONV_HALO, CONV_WIDTH), F32),
                        pltpu.VMEM((tm + POOL_HALO, POOL_WIDTH), F32)],
        compiler_params=pltpu.CompilerParams(
            dimension_semantics=("arbitrary", "arbitrary"), vmem_limit_bytes=VMEM_LIMIT),
        name="mixer",
    )(x, *consts)


def _ffn_kernel(x_ref, g_ref, wup_ref, wdn_ref, o_ref):
    x = x_ref[...]
    hb = _rmsnorm(x, g_ref[...]).astype(BF16)
    a = _dot(hb, wup_ref[:, :D_FF])
    b = _dot(hb, wup_ref[:, D_FF:])
    act = (a * jax.nn.sigmoid(a) * b).astype(BF16)
    o_ref[...] = x + _dot(act, wdn_ref[...])


def _dense_ffn(x2, g, w_up, w_down):
    T, D = x2.shape
    tm = TM_FFN
    x_spec = pl.BlockSpec((tm, D), lambda i: (i, 0))
    return pl.pallas_call(
        _ffn_kernel,
        grid=(T // tm,),
        in_specs=[x_spec, _const_spec(g.shape), _const_spec(w_up.shape), _const_spec(w_down.shape)],
        out_specs=x_spec,
        out_shape=jax.ShapeDtypeStruct(x2.shape, F32),
        compiler_params=pltpu.CompilerParams(
            dimension_semantics=("arbitrary",), vmem_limit_bytes=VMEM_LIMIT),
        name="dense_ffn",
    )(x2, g, w_up, w_down)


def _router_kernel(x_ref, g_ref, wr_ref, h_ref, gate_ref):
    h = _rmsnorm(x_ref[...], g_ref[...])
    h_ref[...] = h.astype(BF16)
    logits = jnp.dot(h, wr_ref[...], preferred_element_type=F32, precision=lax.Precision.HIGHEST)
    lane = lax.broadcasted_iota(jnp.int32, logits.shape, 1)
    neg = jnp.float32(-jnp.inf)
    logits = jnp.where(lane < N_EXPERTS, logits, neg)
    m1 = jnp.max(logits, axis=-1, keepdims=True)
    i1 = jnp.min(jnp.where(logits == m1, lane, LANES), axis=-1, keepdims=True)
    rest = jnp.where(lane == i1, neg, logits)
    m2 = jnp.max(rest, axis=-1, keepdims=True)
    i2 = jnp.min(jnp.where(rest == m2, lane, LANES), axis=-1, keepdims=True)
    e2 = jnp.exp(m2 - m1)
    w1 = 1.0 / (1.0 + e2)
    w2 = e2 / (1.0 + e2)
    gate_ref[...] = jnp.where(lane == i1, w1, 0.0) + jnp.where(lane == i2, w2, 0.0)


def _router(x2, g, wr_pad):
    T, D = x2.shape
    tm = TM_FFN
    return pl.pallas_call(
        _router_kernel,
        grid=(T // tm,),
        in_specs=[pl.BlockSpec((tm, D), lambda i: (i, 0)), _const_spec(g.shape), _const_spec(wr_pad.shape)],
        out_specs=[pl.BlockSpec((tm, D), lambda i: (i, 0)), pl.BlockSpec((tm, LANES), lambda i: (i, 0))],
        out_shape=[jax.ShapeDtypeStruct((T, D), BF16), jax.ShapeDtypeStruct((T, LANES), F32)],
        compiler_params=pltpu.CompilerParams(
            dimension_semantics=("arbitrary",), vmem_limit_bytes=VMEM_LIMIT),
        name="router",
    )(x2, g, wr_pad)


def _moe_kernel(h_ref, gate_ref, wa_ref, wb_ref, wd_ref, x_ref, gf_ref, o_ref, acc_ref):
    e, f = pl.program_id(1), pl.program_id(2)
    first = jnp.logical_and(e == 0, f == 0)
    last = jnp.logical_and(e == pl.num_programs(1) - 1, f == pl.num_programs(2) - 1)

    @pl.when(first)
    def _():
        acc_ref[...] = jnp.zeros_like(acc_ref)

    hb = h_ref[...]
    a = _dot(hb, wa_ref[0])
    b = _dot(hb, wb_ref[0])
    act = (a * jax.nn.sigmoid(a) * b).astype(BF16)
    acc_ref[...] += gate_ref[0] * _dot(act, wd_ref[0])

    @pl.when(last)
    def _():
        o_ref[...] = _rmsnorm(x_ref[...] + acc_ref[...], gf_ref[...])


def _moe_final(x2, hb, gate_et, w_up, w_down, g_final):
    T, D = x2.shape
    tm, tf = TM_MOE, TF_MOE
    nf = D_FF_EXPERT // tf
    row_spec = lambda: pl.BlockSpec((tm, D), lambda i, e, f: (i, 0))
    return pl.pallas_call(
        _moe_kernel,
        grid=(T // tm, N_EXPERTS, nf),
        in_specs=[row_spec(),
                  pl.BlockSpec((1, tm, 1), lambda i, e, f: (e, i, 0)),
                  pl.BlockSpec((1, D, tf), lambda i, e, f: (e, 0, f)),
                  pl.BlockSpec((1, D, tf), lambda i, e, f: (e, 0, f + nf)),
                  pl.BlockSpec((1, tf, D), lambda i, e, f: (e, f, 0)),
                  row_spec(),
                  _const_spec(g_final.shape)],
        out_specs=row_spec(),
        out_shape=jax.ShapeDtypeStruct((T, D), F32),
        scratch_shapes=[pltpu.VMEM((tm, D), F32)],
        compiler_params=pltpu.CompilerParams(
            dimension_semantics=("arbitrary", "arbitrary", "arbitrary"), vmem_limit_bytes=VMEM_LIMIT),
        name="moe",
    )(hb, gate_et, w_up, w_up, w_down, x2, g_final)


def kernel(x, norm_mix, w_in, b_in, conv_w, conv_b, conv_ln_g, conv_ln_b, conv_out, pool_mix, pool_scale,
           pool_out, sgu_ln_g, sgu_ln_b, sgu_w, sgu_b, sgu_out, w_gate, b_gate, w_o, norm_ffn, ffn_w_up,
           ffn_w_down, moe_router, moe_w_up, moe_w_down, norm_final):
    B, S, D = x.shape
    row = lambda a: a.reshape(1, -1)
    for i in range(DEPTH):
        params = {
            "norm": row(norm_mix[i]), "w_in": w_in[i].astype(BF16), "b_in": row(b_in[i]),
            "conv_w": conv_w[i], "conv_b": row(conv_b[i]),
            "conv_ln_g": row(conv_ln_g[i]), "conv_ln_b": row(conv_ln_b[i]),
            "conv_out": conv_out[i].astype(BF16),
            "pool_mix": pool_mix[i].astype(BF16), "pool_scale": row(pool_scale[i]),
            "pool_out": pool_out[i].astype(BF16),
            "sgu_ln_g": row(sgu_ln_g[i]), "sgu_ln_b": row(sgu_ln_b[i]), "sgu_w": sgu_w[i],
            "sgu_b": jnp.repeat(sgu_b[i].T, SGU_HEAD_DIM, axis=1),
            "sgu_out": sgu_out[i].astype(BF16),
            "w_gate": w_gate[i].astype(BF16), "b_gate": row(b_gate[i]), "w_o": w_o[i].astype(BF16),
        }
        x = _mixer_layer(x, params)
        x2 = x.reshape(B * S, D)
        if i % 2 == 0:
            x2 = _dense_ffn(x2, row(norm_ffn[i]), ffn_w_up[i // 2].astype(BF16),
                            ffn_w_down[i // 2].astype(BF16))
            x = x2.reshape(B, S, D)
        else:
            wr_pad = jnp.pad(moe_router[i // 2], ((0, 0), (0, LANES - N_EXPERTS)))
            hb, gate = _router(x2, row(norm_ffn[i]), wr_pad)
            gate_et = gate[:, :N_EXPERTS].T[:, :, None]
            x2 = _moe_final(x2, hb, gate_et, moe_w_up[i // 2].astype(BF16),
                            moe_w_down[i // 2].astype(BF16), row(norm_final))
            x = x2.reshape(B, S, D)
    return x
```

```python
import functools

import jax
import jax.numpy as jnp
from jax import lax
from jax.experimental import pallas as pl
from jax.experimental.pallas import tpu as pltpu

D_MODEL = 1024
DEPTH = 2
CONV_WIDTH = 512
CONV_KERNEL = 31
POOL_WINDOWS = (2, 4, 8, 16)
POOL_GROUP = 128
POOL_WIDTH = POOL_GROUP * len(POOL_WINDOWS)
SGU_HEADS = 4
SGU_HEAD_DIM = 128
SGU_WIDTH = SGU_HEADS * SGU_HEAD_DIM
SGU_BLOCK = 128
N_BRANCH = 3
IN_WIDTH = 2 * CONV_WIDTH + POOL_WIDTH + 2 * SGU_WIDTH
D_FF = 2816
N_EXPERTS = 8
TOP_K = 2
D_FF_EXPERT = 3584
EPS = 1e-6

LANES = 128
CONV_HALO = 32
POOL_HALO = 16
VMEM_LIMIT = 56 * 1024 * 1024

TM_MIX = 512
CONV_ROWS = 64
TM_FFN = 512
TM_ROUTE = 512
TM_MOE = 1024
TF_MOE = 512
D_PACK = D_MODEL // 2

F32 = jnp.float32
BF16 = jnp.bfloat16
U32 = jnp.uint32
HI_MASK = 0xFFFF0000


def _const_spec(shape):
    zeros = (0,) * len(shape)
    return pl.BlockSpec(shape, lambda *_: zeros, pipeline_mode=pl.Buffered(1))


def _rmsnorm(x, g):
    return x * lax.rsqrt(jnp.mean(x * x, axis=-1, keepdims=True) + EPS) * g


def _layernorm(x, g, b):
    mu = jnp.mean(x, axis=-1, keepdims=True)
    xc = x - mu
    var = jnp.mean(xc * xc, axis=-1, keepdims=True)
    return xc * lax.rsqrt(var + EPS) * g + b


def _dot(a, b):
    return jnp.dot(a, b, preferred_element_type=F32)


def _pack_bf16_pairs(v):
    k = v.shape[1] // 2
    bits = pltpu.bitcast(v.astype(BF16).astype(F32), U32)
    return (bits[:, k:] & jnp.uint32(HI_MASK)) | (bits[:, :k] >> 16)


def _unpack_bf16_pairs(w):
    lo = pltpu.bitcast(w << 16, F32)
    hi = pltpu.bitcast(w & jnp.uint32(HI_MASK), F32)
    return lo, hi


def _mixer_kernel(x_ref, g_ref, w_in_ref, b_in_ref, cw_ref, cb_ref, clg_ref, clb_ref, cout_ref,
                  pmix_ref, pscale_ref, pout_ref, slg_ref, slb_ref, sw_ref, sb_ref, sout_ref,
                  wg_ref, bg_ref, wo_ref, o_ref, conv_ext, pool_ext):
    tm = x_ref.shape[1]
    j = pl.program_id(1)
    x = x_ref[0]
    hb = _rmsnorm(x, g_ref[...]).astype(BF16)

    def proj(lo, hi):
        return _dot(hb, w_in_ref[:, lo:hi]) + b_in_ref[:, lo:hi]

    def gate(k):
        lo, hi = k * D_MODEL, (k + 1) * D_MODEL
        return jax.nn.sigmoid(_dot(hb, wg_ref[:, lo:hi]) + bg_ref[:, lo:hi])

    @pl.when(j == 0)
    def _():
        conv_ext[0:CONV_HALO, :] = jnp.zeros((CONV_HALO, CONV_WIDTH), F32)
        pool_ext[0:POOL_HALO, :] = jnp.zeros((POOL_HALO, POOL_WIDTH), F32)

    @pl.when(j > 0)
    def _():
        conv_ext[0:CONV_HALO, :] = conv_ext[tm:tm + CONV_HALO, :]
        pool_ext[0:POOL_HALO, :] = pool_ext[tm:tm + POOL_HALO, :]

    c0 = 0
    conv_ext[CONV_HALO:, :] = proj(c0, c0 + CONV_WIDTH) * jax.nn.sigmoid(
        proj(c0 + CONV_WIDTH, c0 + 2 * CONV_WIDTH))
    base = CONV_HALO - (CONV_KERNEL - 1)
    chunks = []
    for r in range(0, tm, CONV_ROWS):
        acc = jnp.zeros((CONV_ROWS, CONV_WIDTH), F32) + cb_ref[...]
        for k in range(CONV_KERNEL):
            acc = acc + conv_ext[r + base + k:r + base + k + CONV_ROWS, :] * cw_ref[k:k + 1, :]
        hn = _layernorm(acc, clg_ref[...], clb_ref[...])
        chunks.append((hn * jax.nn.sigmoid(hn)).astype(BF16))
    y_a = _dot(jnp.concatenate(chunks, axis=0), cout_ref[...])
    merged = gate(0) * y_a

    c1 = 2 * CONV_WIDTH
    p = proj(c1, c1 + POOL_WIDTH)
    pool_ext[POOL_HALO:, :] = p
    pos = j * tm + lax.broadcasted_iota(jnp.int32, (tm, 1), 0)
    mixed = []
    for gi, w in enumerate(POOL_WINDOWS):
        lo, hi = gi * POOL_GROUP, (gi + 1) * POOL_GROUP
        s = pool_ext[POOL_HALO:POOL_HALO + tm, lo:hi]
        for i in range(1, w):
            s = s + pool_ext[POOL_HALO - i:POOL_HALO - i + tm, lo:hi]
        cnt = jnp.minimum(pos + 1, w).astype(F32)
        d = s / cnt - p[:, lo:hi]
        mixed.append(_dot(d.astype(BF16), pmix_ref[gi]))
    yb = jnp.concatenate(mixed, axis=-1) * pscale_ref[...]
    merged = merged + gate(1) * _dot(yb.astype(BF16), pout_ref[...])

    c2 = c1 + POOL_WIDTH
    u = jax.nn.gelu(proj(c2, c2 + SGU_WIDTH), approximate=True)
    v = jax.nn.gelu(proj(c2 + SGU_WIDTH, c2 + 2 * SGU_WIDTH), approximate=True)
    vn = _layernorm(v, slg_ref[...], slb_ref[...]).astype(BF16)
    row = lax.broadcasted_iota(jnp.int32, (SGU_BLOCK, SGU_BLOCK), 0)
    col = lax.broadcasted_iota(jnp.int32, (SGU_BLOCK, SGU_BLOCK), 1)
    ws = [jnp.where(row >= col, sw_ref[hd], 0.0).astype(BF16) for hd in range(SGU_HEADS)]
    blocks = []
    for r in range(0, tm, SGU_BLOCK):
        heads = [_dot(ws[hd], vn[r:r + SGU_BLOCK, hd * SGU_HEAD_DIM:(hd + 1) * SGU_HEAD_DIM])
                 for hd in range(SGU_HEADS)]
        blocks.append(jnp.concatenate(heads, axis=-1) + sb_ref[...])
    y = u * jnp.concatenate(blocks, axis=0)
    merged = merged + gate(2) * _dot(y.astype(BF16), sout_ref[...])

    o_ref[0] = x + _dot(merged.astype(BF16), wo_ref[...])


def _mixer_layer(x, p):
    B, S, D = x.shape
    tm = TM_MIX
    consts = [p["norm"], p["w_in"], p["b_in"], p["conv_w"], p["conv_b"], p["conv_ln_g"], p["conv_ln_b"],
              p["conv_out"], p["pool_mix"], p["pool_scale"], p["pool_out"], p["sgu_ln_g"], p["sgu_ln_b"],
              p["sgu_w"], p["sgu_b"], p["sgu_out"], p["w_gate"], p["b_gate"], p["w_o"]]
    x_spec = pl.BlockSpec((1, tm, D), lambda b, j: (b, j, 0))
    return pl.pallas_call(
        _mixer_kernel,
        grid=(B, S // tm),
        in_specs=[x_spec] + [_const_spec(c.shape) for c in consts],
        out_specs=x_spec,
        out_shape=jax.ShapeDtypeStruct(x.shape, F32),
        scratch_shapes=[pltpu.VMEM((tm + CONV_HALO, CONV_WIDTH), F32),
                        pltpu.VMEM((tm + POOL_HALO, POOL_WIDTH), F32)],
        compiler_params=pltpu.CompilerParams(
            dimension_semantics=("arbitrary", "arbitrary"), vmem_limit_bytes=VMEM_LIMIT),
        name="mixer",
    )(x, *consts)


def _ffn_kernel(x_ref, g_ref, wup_ref, wdn_ref, o_ref):
    x = x_ref[...]
    hb = _rmsnorm(x, g_ref[...]).astype(BF16)
    a = _dot(hb, wup_ref[:, :D_FF])
    b = _dot(hb, wup_ref[:, D_FF:])
    act = (a * jax.nn.sigmoid(a) * b).astype(BF16)
    o_ref[...] = x + _dot(act, wdn_ref[...])


def _dense_ffn(x2, g, w_up, w_down):
    T, D = x2.shape
    tm = TM_FFN
    x_spec = pl.BlockSpec((tm, D), lambda i: (i, 0))
    return pl.pallas_call(
        _ffn_kernel,
        grid=(T // tm,),
        in_specs=[x_spec, _const_spec(g.shape), _const_spec(w_up.shape), _const_spec(w_down.shape)],
        out_specs=x_spec,
        out_shape=jax.ShapeDtypeStruct(x2.shape, F32),
        compiler_params=pltpu.CompilerParams(
            dimension_semantics=("arbitrary",), vmem_limit_bytes=VMEM_LIMIT),
        name="dense_ffn",
    )(x2, g, w_up, w_down)


def _router_kernel(x_ref, g_ref, wr_ref, hp_ref, wts_ref, pos_ref, cnt_ref, carry_ref, *, cap):
    tm = x_ref.shape[0]

    @pl.when(pl.program_id(0) == 0)
    def _():
        carry_ref[...] = jnp.zeros_like(carry_ref)

    h = _rmsnorm(x_ref[...], g_ref[...])
    hp_ref[...] = _pack_bf16_pairs(h)
    logits = jnp.dot(h, wr_ref[...], preferred_element_type=F32, precision=lax.Precision.HIGHEST)
    lane = lax.broadcasted_iota(jnp.int32, logits.shape, 1)
    neg = jnp.float32(-jnp.inf)
    logits = jnp.where(lane < N_EXPERTS, logits, neg)
    m1 = jnp.max(logits, axis=-1, keepdims=True)
    i1 = jnp.min(jnp.where(logits == m1, lane, LANES), axis=-1, keepdims=True)
    rest = jnp.where(lane == i1, neg, logits)
    m2 = jnp.max(rest, axis=-1, keepdims=True)
    i2 = jnp.min(jnp.where(rest == m2, lane, LANES), axis=-1, keepdims=True)
    e2 = jnp.exp(m2 - m1)
    w1 = 1.0 / (1.0 + e2)
    w2 = e2 / (1.0 + e2)

    chosen = jnp.where(jnp.logical_or(lane == i1, lane == i2), 1.0, 0.0)
    r_io = lax.broadcasted_iota(jnp.int32, (tm, tm), 0)
    c_io = lax.broadcasted_iota(jnp.int32, (tm, tm), 1)
    tri = jnp.where(r_io >= c_io, 1.0, 0.0).astype(BF16)
    incl = _dot(tri, chosen.astype(BF16))
    before = incl - chosen + carry_ref[...]
    rank1 = jnp.sum(jnp.where(lane == i1, before, 0.0), axis=-1, keepdims=True).astype(jnp.int32)
    rank2 = jnp.sum(jnp.where(lane == i2, before, 0.0), axis=-1, keepdims=True).astype(jnp.int32)
    carry_ref[...] = carry_ref[...] + incl[tm - 1:tm, :]
    cnt_ref[...] = carry_ref[...].astype(jnp.int32)

    wts_ref[...] = jnp.where(lane == 0, w1, jnp.where(lane == 1, w2, 0.0))
    pos_ref[...] = jnp.where(lane == 0, i1 * cap + rank1, jnp.where(lane == 1, i2 * cap + rank2, 0))


def _router(x2, g, wr_pad, cap):
    T, D = x2.shape
    tm = TM_ROUTE
    tile = lambda w: pl.BlockSpec((tm, w), lambda i: (i, 0))
    return pl.pallas_call(
        functools.partial(_router_kernel, cap=cap),
        grid=(T // tm,),
        in_specs=[tile(D), _const_spec(g.shape), _const_spec(wr_pad.shape)],
        out_specs=[tile(D_PACK), tile(LANES), tile(LANES), pl.BlockSpec((1, LANES), lambda i: (0, 0))],
        out_shape=[jax.ShapeDtypeStruct((T, D_PACK), U32), jax.ShapeDtypeStruct((T, LANES), F32),
                   jax.ShapeDtypeStruct((T, LANES), jnp.int32), jax.ShapeDtypeStruct((1, LANES), jnp.int32)],
        scratch_shapes=[pltpu.VMEM((1, LANES), F32)],
        compiler_params=pltpu.CompilerParams(
            dimension_semantics=("arbitrary",), vmem_limit_bytes=VMEM_LIMIT),
        name="router",
    )(x2, g, wr_pad)


def _row_copy(src, src_row, dst, dst_row, sem):
    return pltpu.make_async_copy(src.at[pl.ds(src_row, 1), :], dst.at[pl.ds(dst_row, 1), :], sem)


def _dispatch_kernel(pos_ref, hp_ref, xs_ref, sem):
    tm = hp_ref.shape[0]

    def issue(r, carry):
        _row_copy(hp_ref, r, xs_ref, pos_ref[r], sem).start()
        _row_copy(hp_ref, r, xs_ref, pos_ref[tm + r], sem).start()
        return carry

    lax.fori_loop(0, tm, issue, 0, unroll=8)
    for _ in range(TOP_K):
        pltpu.make_async_copy(hp_ref, xs_ref.at[pl.ds(0, tm), :], sem).wait()


def _dispatch(pos_flat, hp, n_rows):
    T, dp = hp.shape
    tm = TM_ROUTE
    return pl.pallas_call(
        _dispatch_kernel,
        grid=(T // tm,),
        in_specs=[pl.BlockSpec((TOP_K * tm,), lambda i: (i,), memory_space=pltpu.SMEM),
                  pl.BlockSpec((tm, dp), lambda i: (i, 0))],
        out_specs=pl.BlockSpec(memory_space=pl.ANY),
        out_shape=jax.ShapeDtypeStruct((n_rows, dp), U32),
        scratch_shapes=[pltpu.SemaphoreType.DMA(())],
        compiler_params=pltpu.CompilerParams(
            dimension_semantics=("arbitrary",), vmem_limit_bytes=VMEM_LIMIT, has_side_effects=True),
        name="dispatch",
    )(pos_flat, hp)


def _experts_kernel(te_ref, tb_ref, tv_ref, xs_ref, wa_ref, wb_ref, wd_ref, ys_ref, xb_ref, acc_ref):
    n, f = pl.program_id(0), pl.program_id(1)
    tm = xs_ref.shape[0]
    valid = tv_ref[n]

    @pl.when(valid > 0)
    def _():
        @pl.when(f == 0)
        def _():
            keep = lax.broadcasted_iota(jnp.int32, (tm, D_PACK), 0) < valid
            lo, hi = _unpack_bf16_pairs(jnp.where(keep, xs_ref[...], jnp.uint32(0)))
            xb_ref[:, :D_PACK] = lo.astype(BF16)
            xb_ref[:, D_PACK:] = hi.astype(BF16)
            acc_ref[...] = jnp.zeros_like(acc_ref)

        xb = xb_ref[...]
        a = _dot(xb, wa_ref[0])
        b = _dot(xb, wb_ref[0])
        act = (a * jax.nn.sigmoid(a) * b).astype(BF16)
        acc_ref[...] += _dot(act, wd_ref[0])

        @pl.when(f == pl.num_programs(1) - 1)
        def _():
            ys_ref[...] = _pack_bf16_pairs(acc_ref[...])


def _experts(tile_e, tile_blk, tile_valid, xs, w_up, w_down):
    n_rows, dp = xs.shape
    tm, tf = TM_MOE, TF_MOE
    nf = D_FF_EXPERT // tf
    nt = tile_e.shape[0]

    def fcol(n, f, tv):
        return jnp.where(tv[n] > 0, f, nf - 1)

    rows = pl.BlockSpec((tm, dp), lambda n, f, te, tb, tv: (tb[n], 0))
    grid_spec = pltpu.PrefetchScalarGridSpec(
        num_scalar_prefetch=3,
        grid=(nt, nf),
        in_specs=[rows,
                  pl.BlockSpec((1, D_MODEL, tf), lambda n, f, te, tb, tv: (te[n], 0, fcol(n, f, tv))),
                  pl.BlockSpec((1, D_MODEL, tf), lambda n, f, te, tb, tv: (te[n], 0, fcol(n, f, tv) + nf)),
                  pl.BlockSpec((1, tf, D_MODEL), lambda n, f, te, tb, tv: (te[n], fcol(n, f, tv), 0))],
        out_specs=rows,
        scratch_shapes=[pltpu.VMEM((tm, D_MODEL), BF16), pltpu.VMEM((tm, D_MODEL), F32)],
    )
    return pl.pallas_call(
        _experts_kernel,
        grid_spec=grid_spec,
        out_shape=jax.ShapeDtypeStruct((n_rows, dp), U32),
        compiler_params=pltpu.CompilerParams(
            dimension_semantics=("arbitrary", "arbitrary"), vmem_limit_bytes=VMEM_LIMIT),
        name="experts",
    )(tile_e, tile_blk, tile_valid, xs, w_up, w_up, w_down)


def _combine_kernel(pos_ref, wts_ref, x_ref, gf_ref, ys_ref, o_ref, y1_ref, y2_ref, sem):
    tm = x_ref.shape[0]

    def issue(r, carry):
        _row_copy(ys_ref, pos_ref[r], y1_ref, r, sem).start()
        _row_copy(ys_ref, pos_ref[tm + r], y2_ref, r, sem).start()
        return carry

    lax.fori_loop(0, tm, issue, 0, unroll=8)
    for buf in (y1_ref, y2_ref):
        pltpu.make_async_copy(ys_ref.at[pl.ds(0, tm), :], buf, sem).wait()

    w1 = wts_ref[:, 0:1]
    w2 = wts_ref[:, 1:2]
    lo1, hi1 = _unpack_bf16_pairs(y1_ref[...])
    lo2, hi2 = _unpack_bf16_pairs(y2_ref[...])
    moe = jnp.concatenate([w1 * lo1 + w2 * lo2, w1 * hi1 + w2 * hi2], axis=-1)
    o_ref[...] = _rmsnorm(x_ref[...] + moe, gf_ref[...])


def _combine(pos_flat, wts, x2, g_final, ys):
    T, D = x2.shape
    tm = TM_ROUTE
    return pl.pallas_call(
        _combine_kernel,
        grid=(T // tm,),
        in_specs=[pl.BlockSpec((TOP_K * tm,), lambda i: (i,), memory_space=pltpu.SMEM),
                  pl.BlockSpec((tm, LANES), lambda i: (i, 0)),
                  pl.BlockSpec((tm, D), lambda i: (i, 0)),
                  _const_spec(g_final.shape),
                  pl.BlockSpec(memory_space=pl.ANY)],
        out_specs=pl.BlockSpec((tm, D), lambda i: (i, 0)),
        out_shape=jax.ShapeDtypeStruct((T, D), F32),
        scratch_shapes=[pltpu.VMEM((tm, D_PACK), U32), pltpu.VMEM((tm, D_PACK), U32),
                        pltpu.SemaphoreType.DMA(())],
        compiler_params=pltpu.CompilerParams(
            dimension_semantics=("arbitrary",), vmem_limit_bytes=VMEM_LIMIT),
        name="combine",
    )(pos_flat, wts, x2, g_final, ys)


def _tile_plan(counts, cap, tm, nt):
    tiles_e = (counts + tm - 1) // tm
    ends = jnp.cumsum(tiles_e)
    total = ends[-1]
    n = jnp.minimum(jnp.arange(nt, dtype=jnp.int32), total - 1)
    e = jnp.minimum(jnp.searchsorted(ends, n, side="right").astype(jnp.int32), N_EXPERTS - 1)
    local = n - (ends[e] - tiles_e[e])
    blk = e * (cap // tm) + local
    valid = jnp.where(jnp.arange(nt) < total, jnp.clip(counts[e] - local * tm, 0, tm), 0)
    return e, blk.astype(jnp.int32), valid.astype(jnp.int32)


def _moe_layer(x2, g_ffn, w_router, w_up, w_down, g_final):
    T, D = x2.shape
    cap = T
    wr_pad = jnp.pad(w_router, ((0, 0), (0, LANES - N_EXPERTS)))
    hp, wts, pos, cnt = _router(x2, g_ffn, wr_pad, cap)
    pos_flat = pos[:, :TOP_K].reshape(T // TM_ROUTE, TM_ROUTE, TOP_K).transpose(0, 2, 1).reshape(-1)
    xs = _dispatch(pos_flat, hp, N_EXPERTS * cap)
    nt = TOP_K * T // TM_MOE + N_EXPERTS
    tile_e, tile_blk, tile_valid = _tile_plan(cnt[0, :N_EXPERTS], cap, TM_MOE, nt)
    ys = _experts(tile_e, tile_blk, tile_valid, xs, w_up, w_down)
    return _combine(pos_flat, wts, x2, g_final, ys)


def kernel(x, norm_mix, w_in, b_in, conv_w, conv_b, conv_ln_g, conv_ln_b, conv_out, pool_mix, pool_scale,
           pool_out, sgu_ln_g, sgu_ln_b, sgu_w, sgu_b, sgu_out, w_gate, b_gate, w_o, norm_ffn, ffn_w_up,
           ffn_w_down, moe_router, moe_w_up, moe_w_down, norm_final):
    B, S, D = x.shape
    row = lambda a: a.reshape(1, -1)
    for i in range(DEPTH):
        params = {
            "norm": row(norm_mix[i]), "w_in": w_in[i].astype(BF16), "b_in": row(b_in[i]),
            "conv_w": conv_w[i], "conv_b": row(conv_b[i]),
            "conv_ln_g": row(conv_ln_g[i]), "conv_ln_b": row(conv_ln_b[i]),
            "conv_out": conv_out[i].astype(BF16),
            "pool_mix": pool_mix[i].astype(BF16), "pool_scale": row(pool_scale[i]),
            "pool_out": pool_out[i].astype(BF16),
            "sgu_ln_g": row(sgu_ln_g[i]), "sgu_ln_b": row(sgu_ln_b[i]), "sgu_w": sgu_w[i],
            "sgu_b": jnp.repeat(sgu_b[i].T, SGU_HEAD_DIM, axis=1),
            "sgu_out": sgu_out[i].astype(BF16),
            "w_gate": w_gate[i].astype(BF16), "b_gate": row(b_gate[i]), "w_o": w_o[i].astype(BF16),
        }
        x = _mixer_layer(x, params)
        x2 = x.reshape(B * S, D)
        if i % 2 == 0:
            x2 = _dense_ffn(x2, row(norm_ffn[i]), ffn_w_up[i // 2].astype(BF16),
                            ffn_w_down[i // 2].astype(BF16))
        else:
            assert i == DEPTH - 1
            x2 = _moe_layer(x2, row(norm_ffn[i]), moe_router[i // 2], moe_w_up[i // 2].astype(BF16),
                            moe_w_down[i // 2].astype(BF16), row(norm_final))
        x = x2.reshape(B, S, D)
    return x
```

```python
import functools

import jax
import jax.numpy as jnp
from jax import lax
from jax.experimental import pallas as pl
from jax.experimental.pallas import tpu as pltpu

D_MODEL = 1024
DEPTH = 2
CONV_WIDTH = 512
CONV_KERNEL = 31
POOL_WINDOWS = (2, 4, 8, 16)
POOL_GROUP = 128
POOL_WIDTH = POOL_GROUP * len(POOL_WINDOWS)
SGU_HEADS = 4
SGU_HEAD_DIM = 128
SGU_WIDTH = SGU_HEADS * SGU_HEAD_DIM
SGU_BLOCK = 128
N_BRANCH = 3
IN_WIDTH = 2 * CONV_WIDTH + POOL_WIDTH + 2 * SGU_WIDTH
D_FF = 2816
N_EXPERTS = 8
TOP_K = 2
D_FF_EXPERT = 3584
EPS = 1e-6

LANES = 128
SUBLANES = 8
MXU_COLS = 256
N_Z_PIECES = (POOL_WIDTH + 2 * SGU_WIDTH) // MXU_COLS
CONV_HALO = 32
POOL_HALO = 16
VMEM_LIMIT = 56 * 1024 * 1024

TM_MIX = 512
CONV_ROWS = 64
TM_FFN = 512
TM_ROUTE = 512
TM_MOE = 1024
TF_MOE = 512
D_PACK = D_MODEL // 2

F32 = jnp.float32
BF16 = jnp.bfloat16
U32 = jnp.uint32
HI_MASK = 0xFFFF0000


def _const_spec(shape):
    zeros = (0,) * len(shape)
    return pl.BlockSpec(shape, lambda *_: zeros, pipeline_mode=pl.Buffered(1))


def _rmsnorm(x, g):
    return x * lax.rsqrt(jnp.mean(x * x, axis=-1, keepdims=True) + EPS) * g


def _layernorm(x, g, b):
    mu = jnp.mean(x, axis=-1, keepdims=True)
    xc = x - mu
    var = jnp.mean(xc * xc, axis=-1, keepdims=True)
    return xc * lax.rsqrt(var + EPS) * g + b


def _dot(a, b):
    return jnp.dot(a, b, preferred_element_type=F32)


def _pack_bf16_pairs(v):
    k = v.shape[1] // 2
    bits = pltpu.bitcast(v.astype(BF16).astype(F32), U32)
    return (bits[:, k:] & jnp.uint32(HI_MASK)) | (bits[:, :k] >> 16)


def _unpack_bf16_pairs(w):
    lo = pltpu.bitcast(w << 16, F32)
    hi = pltpu.bitcast(w & jnp.uint32(HI_MASK), F32)
    return lo, hi


def _mixer_kernel(x_ref, g_ref, wc_ref, bc_ref, wzg_ref, bzg_ref, cw_ref, cb_ref, clg_ref, clb_ref, cout_ref,
                  pmix_ref, pscale_ref, pout_ref, slg_ref, slb_ref, sw_ref, sb_ref, sout_ref, wo_ref,
                  o_ref, hb_buf, conv_ext, conv_sh, pool_ext, zg_buf, act_buf, m_buf):
    tm = x_ref.shape[1]
    j = pl.program_id(1)
    x = x_ref[0]
    hb_buf[...] = _rmsnorm(x, g_ref[...]).astype(BF16)

    def proj(lo, hi):
        return _dot(hb_buf[...], wc_ref[:, lo:hi]) + bc_ref[:, lo:hi]

    @pl.when(j == 0)
    def _():
        conv_ext[0:CONV_HALO, :] = jnp.zeros((CONV_HALO, CONV_WIDTH), F32)
        pool_ext[0:POOL_HALO, :] = jnp.zeros((POOL_HALO, POOL_WIDTH), F32)

    @pl.when(j > 0)
    def _():
        conv_ext[0:CONV_HALO, :] = conv_ext[tm:tm + CONV_HALO, :]
        pool_ext[0:POOL_HALO, :] = pool_ext[tm:tm + POOL_HALO, :]

    conv_ext[CONV_HALO:, :] = proj(0, CONV_WIDTH) * jax.nn.sigmoid(proj(CONV_WIDTH, 2 * CONV_WIDTH))

    n_groups = conv_sh.shape[1]
    for m in range(SUBLANES):
        groups = n_groups - (1 if m else 0)
        conv_sh[m, 0:groups] = conv_ext[m:m + groups * SUBLANES, :].reshape(groups, SUBLANES, CONV_WIDTH)

    base = CONV_HALO - (CONV_KERNEL - 1)
    gpc = CONV_ROWS // SUBLANES

    def conv_chunk(ci):
        acc = jnp.zeros((gpc, SUBLANES, CONV_WIDTH), F32) + cb_ref[...]
        for k in range(CONV_KERNEL):
            q, m = divmod(base + k, SUBLANES)
            acc = acc + conv_sh[m, pl.ds(ci * gpc + q, gpc)] * cw_ref[k]
        hn = _layernorm(acc.reshape(CONV_ROWS, CONV_WIDTH), clg_ref[...], clb_ref[...])
        rows = pl.ds(pl.multiple_of(ci * CONV_ROWS, CONV_ROWS), CONV_ROWS)
        act_buf[rows, :] = (hn * jax.nn.sigmoid(hn)).astype(BF16)

    def piece(c, act):
        zg_buf[c] = act(_dot(hb_buf[...], wzg_ref[c]) + bzg_ref[c])

    n_chunks = tm // CONV_ROWS
    n_pieces = zg_buf.shape[0]
    assert N_Z_PIECES <= n_chunks <= n_pieces

    def chunk_and_proj(i, carry):
        conv_chunk(i)
        piece(i, lambda z: z)
        return carry

    def chunk_and_gate(i, carry):
        conv_chunk(i)
        piece(i, jax.nn.sigmoid)
        return carry

    for i in range(N_Z_PIECES):
        chunk_and_proj(i, 0)
    for i in range(N_Z_PIECES, n_chunks):
        chunk_and_gate(i, 0)
    for c in range(n_chunks, n_pieces):
        piece(c, jax.nn.sigmoid)

    def z_cols(lo, hi):
        return jnp.concatenate([zg_buf[c] for c in range(lo // MXU_COLS, hi // MXU_COLS)], axis=-1)

    def gate(k):
        c0 = N_Z_PIECES + k * (D_MODEL // MXU_COLS)
        return jnp.concatenate([zg_buf[c] for c in range(c0, c0 + D_MODEL // MXU_COLS)], axis=-1)

    p = z_cols(0, POOL_WIDTH)
    pool_ext[POOL_HALO:, :] = p
    head_pos = lax.broadcasted_iota(jnp.int32, (POOL_HALO, POOL_GROUP), 0)
    diffs = []
    for gi, w in enumerate(POOL_WINDOWS):
        lo, hi = gi * POOL_GROUP, (gi + 1) * POOL_GROUP
        s = pool_ext[:, lo:hi]
        step = 1
        while step < w:
            s = s + pltpu.roll(s, step, axis=0)
            step *= 2
        s = s[POOL_HALO:, :]
        mean = s * (1.0 / w)
        head_cnt = jnp.minimum(head_pos + 1, w).astype(F32)
        head = jnp.where(j == 0, s[:POOL_HALO, :] / head_cnt, mean[:POOL_HALO, :])
        mean = jnp.concatenate([head, mean[POOL_HALO:, :]], axis=0)
        diffs.append((mean - p[:, lo:hi]).astype(BF16))
    u = jax.nn.gelu(z_cols(POOL_WIDTH, POOL_WIDTH + SGU_WIDTH), approximate=True)
    mixed = [_dot(diffs[gi], pmix_ref[gi]) for gi in range(len(POOL_WINDOWS))]
    yb = jnp.concatenate(mixed, axis=-1) * pscale_ref[...]
    v = jax.nn.gelu(z_cols(POOL_WIDTH + SGU_WIDTH, POOL_WIDTH + 2 * SGU_WIDTH), approximate=True)
    vn = _layernorm(v, slg_ref[...], slb_ref[...]).astype(BF16)
    m_buf[...] = gate(1) * _dot(yb.astype(BF16), pout_ref[...])

    row = lax.broadcasted_iota(jnp.int32, (SGU_BLOCK, SGU_BLOCK), 0)
    col = lax.broadcasted_iota(jnp.int32, (SGU_BLOCK, SGU_BLOCK), 1)
    ws = [jnp.where(row >= col, sw_ref[hd], 0.0).astype(BF16) for hd in range(SGU_HEADS)]
    blocks = []
    for r in range(0, tm, SGU_BLOCK):
        heads = [_dot(ws[hd], vn[r:r + SGU_BLOCK, hd * SGU_HEAD_DIM:(hd + 1) * SGU_HEAD_DIM])
                 for hd in range(SGU_HEADS)]
        blocks.append(jnp.concatenate(heads, axis=-1) + sb_ref[...])
    y = u * jnp.concatenate(blocks, axis=0)
    m_buf[...] += gate(2) * _dot(y.astype(BF16), sout_ref[...])
    merged = m_buf[...] + gate(0) * _dot(act_buf[...], cout_ref[...])

    o_ref[0] = x + _dot(merged.astype(BF16), wo_ref[...])


def _mixer_layer(x, p):
    B, S, D = x.shape
    tm = TM_MIX
    consts = [p["norm"], p["w_conv_in"], p["b_conv_in"], p["w_zg"], p["b_zg"], p["conv_w"], p["conv_b"],
              p["conv_ln_g"], p["conv_ln_b"], p["conv_out"], p["pool_mix"], p["pool_scale"], p["pool_out"],
              p["sgu_ln_g"], p["sgu_ln_b"], p["sgu_w"], p["sgu_b"], p["sgu_out"], p["w_o"]]
    x_spec = pl.BlockSpec((1, tm, D), lambda b, j: (b, j, 0))
    n_pieces = p["w_zg"].shape[0]
    return pl.pallas_call(
        _mixer_kernel,
        grid=(B, S // tm),
        in_specs=[x_spec] + [_const_spec(c.shape) for c in consts],
        out_specs=x_spec,
        out_shape=jax.ShapeDtypeStruct(x.shape, F32),
        scratch_shapes=[pltpu.VMEM((tm, D), BF16),
                        pltpu.VMEM((tm + CONV_HALO, CONV_WIDTH), F32),
                        pltpu.VMEM((SUBLANES, (tm + CONV_HALO) // SUBLANES, SUBLANES, CONV_WIDTH), F32),
                        pltpu.VMEM((tm + POOL_HALO, POOL_WIDTH), F32),
                        pltpu.VMEM((n_pieces, tm, MXU_COLS), F32),
                        pltpu.VMEM((tm, CONV_WIDTH), BF16),
                        pltpu.VMEM((tm, D), F32)],
        compiler_params=pltpu.CompilerParams(
            dimension_semantics=("arbitrary", "arbitrary"), vmem_limit_bytes=VMEM_LIMIT),
        name="mixer",
    )(x, *consts)


def _ffn_kernel(x_ref, g_ref, wup_ref, wdn_ref, o_ref):
    x = x_ref[...]
    hb = _rmsnorm(x, g_ref[...]).astype(BF16)
    a = _dot(hb, wup_ref[:, :D_FF])
    b = _dot(hb, wup_ref[:, D_FF:])
    act = (a * jax.nn.sigmoid(a) * b).astype(BF16)
    o_ref[...] = x + _dot(act, wdn_ref[...])


def _dense_ffn(x2, g, w_up, w_down):
    T, D = x2.shape
    tm = TM_FFN
    x_spec = pl.BlockSpec((tm, D), lambda i: (i, 0))
    return pl.pallas_call(
        _ffn_kernel,
        grid=(T // tm,),
        in_specs=[x_spec, _const_spec(g.shape), _const_spec(w_up.shape), _const_spec(w_down.shape)],
        out_specs=x_spec,
        out_shape=jax.ShapeDtypeStruct(x2.shape, F32),
        compiler_params=pltpu.CompilerParams(
            dimension_semantics=("arbitrary",), vmem_limit_bytes=VMEM_LIMIT),
        name="dense_ffn",
    )(x2, g, w_up, w_down)


def _router_kernel(x_ref, g_ref, wr_ref, hp_ref, wts_ref, pos_ref, cnt_ref, carry_ref, *, cap):
    tm = x_ref.shape[0]

    @pl.when(pl.program_id(0) == 0)
    def _():
        carry_ref[...] = jnp.zeros_like(carry_ref)

    h = _rmsnorm(x_ref[...], g_ref[...])
    hp_ref[...] = _pack_bf16_pairs(h)
    logits = jnp.dot(h, wr_ref[...], preferred_element_type=F32, precision=lax.Precision.HIGHEST)
    lane = lax.broadcasted_iota(jnp.int32, logits.shape, 1)
    neg = jnp.float32(-jnp.inf)
    logits = jnp.where(lane < N_EXPERTS, logits, neg)
    m1 = jnp.max(logits, axis=-1, keepdims=True)
    i1 = jnp.min(jnp.where(logits == m1, lane, LANES), axis=-1, keepdims=True)
    rest = jnp.where(lane == i1, neg, logits)
    m2 = jnp.max(rest, axis=-1, keepdims=True)
    i2 = jnp.min(jnp.where(rest == m2, lane, LANES), axis=-1, keepdims=True)
    e2 = jnp.exp(m2 - m1)
    w1 = 1.0 / (1.0 + e2)
    w2 = e2 / (1.0 + e2)

    chosen = jnp.where(jnp.logical_or(lane == i1, lane == i2), 1.0, 0.0)
    r_io = lax.broadcasted_iota(jnp.int32, (tm, tm), 0)
    c_io = lax.broadcasted_iota(jnp.int32, (tm, tm), 1)
    tri = jnp.where(r_io >= c_io, 1.0, 0.0).astype(BF16)
    incl = _dot(tri, chosen.astype(BF16))
    before = incl - chosen + carry_ref[...]
    rank1 = jnp.sum(jnp.where(lane == i1, before, 0.0), axis=-1, keepdims=True).astype(jnp.int32)
    rank2 = jnp.sum(jnp.where(lane == i2, before, 0.0), axis=-1, keepdims=True).astype(jnp.int32)
    carry_ref[...] = carry_ref[...] + incl[tm - 1:tm, :]
    cnt_ref[...] = carry_ref[...].astype(jnp.int32)

    wts_ref[...] = jnp.where(lane == 0, w1, jnp.where(lane == 1, w2, 0.0))
    pos_ref[...] = jnp.where(lane == 0, i1 * cap + rank1, jnp.where(lane == 1, i2 * cap + rank2, 0))


def _router(x2, g, wr_pad, cap):
    T, D = x2.shape
    tm = TM_ROUTE
    tile = lambda w: pl.BlockSpec((tm, w), lambda i: (i, 0))
    return pl.pallas_call(
        functools.partial(_router_kernel, cap=cap),
        grid=(T // tm,),
        in_specs=[tile(D), _const_spec(g.shape), _const_spec(wr_pad.shape)],
        out_specs=[tile(D_PACK), tile(LANES), tile(LANES), pl.BlockSpec((1, LANES), lambda i: (0, 0))],
        out_shape=[jax.ShapeDtypeStruct((T, D_PACK), U32), jax.ShapeDtypeStruct((T, LANES), F32),
                   jax.ShapeDtypeStruct((T, LANES), jnp.int32), jax.ShapeDtypeStruct((1, LANES), jnp.int32)],
        scratch_shapes=[pltpu.VMEM((1, LANES), F32)],
        compiler_params=pltpu.CompilerParams(
            dimension_semantics=("arbitrary",), vmem_limit_bytes=VMEM_LIMIT),
        name="router",
    )(x2, g, wr_pad)


def _row_copy(src, src_row, dst, dst_row, sem):
    return pltpu.make_async_copy(src.at[pl.ds(src_row, 1), :], dst.at[pl.ds(dst_row, 1), :], sem)


def _dispatch_kernel(pos_ref, hp_ref, xs_ref, sem):
    tm = hp_ref.shape[0]

    def issue(r, carry):
        _row_copy(hp_ref, r, xs_ref, pos_ref[r], sem).start(priority=0)
        _row_copy(hp_ref, r, xs_ref, pos_ref[tm + r], sem).start(priority=1)
        return carry

    lax.fori_loop(0, tm, issue, 0, unroll=8)
    for _ in range(TOP_K):
        pltpu.make_async_copy(hp_ref, xs_ref.at[pl.ds(0, tm), :], sem).wait()


def _dispatch(pos_flat, hp, n_rows):
    T, dp = hp.shape
    tm = TM_ROUTE
    return pl.pallas_call(
        _dispatch_kernel,
        grid=(T // tm,),
        in_specs=[pl.BlockSpec((TOP_K * tm,), lambda i: (i,), memory_space=pltpu.SMEM),
                  pl.BlockSpec((tm, dp), lambda i: (i, 0))],
        out_specs=pl.BlockSpec(memory_space=pl.ANY),
        out_shape=jax.ShapeDtypeStruct((n_rows, dp), U32),
        scratch_shapes=[pltpu.SemaphoreType.DMA(())],
        compiler_params=pltpu.CompilerParams(
            dimension_semantics=("arbitrary",), vmem_limit_bytes=VMEM_LIMIT, has_side_effects=True),
        name="dispatch",
    )(pos_flat, hp)


def _experts_kernel(te_ref, tb_ref, tv_ref, xs_ref, wa_ref, wb_ref, wd_ref, ys_ref, xb_ref, acc_ref):
    n, f = pl.program_id(0), pl.program_id(1)
    tm = xs_ref.shape[0]
    valid = tv_ref[n]

    @pl.when(valid > 0)
    def _():
        @pl.when(f == 0)
        def _():
            keep = lax.broadcasted_iota(jnp.int32, (tm, D_PACK), 0) < valid
            lo, hi = _unpack_bf16_pairs(jnp.where(keep, xs_ref[...], jnp.uint32(0)))
            xb_ref[:, :D_PACK] = lo.astype(BF16)
            xb_ref[:, D_PACK:] = hi.astype(BF16)
            acc_ref[...] = jnp.zeros_like(acc_ref)

        xb = xb_ref[...]
        a = _dot(xb, wa_ref[0])
        b = _dot(xb, wb_ref[0])
        act = (a * jax.nn.sigmoid(a) * b).astype(BF16)
        acc_ref[...] += _dot(act, wd_ref[0])

        @pl.when(f == pl.num_programs(1) - 1)
        def _():
            ys_ref[...] = _pack_bf16_pairs(acc_ref[...])


def _experts(tile_e, tile_blk, tile_valid, xs, w_up, w_down):
    n_rows, dp = xs.shape
    tm, tf = TM_MOE, TF_MOE
    nf = D_FF_EXPERT // tf
    nt = tile_e.shape[0]

    def fcol(n, f, tv):
        return jnp.where(tv[n] > 0, f, nf - 1)

    rows = pl.BlockSpec((tm, dp), lambda n, f, te, tb, tv: (tb[n], 0))
    grid_spec = pltpu.PrefetchScalarGridSpec(
        num_scalar_prefetch=3,
        grid=(nt, nf),
        in_specs=[rows,
                  pl.BlockSpec((1, D_MODEL, tf), lambda n, f, te, tb, tv: (te[n], 0, fcol(n, f, tv))),
                  pl.BlockSpec((1, D_MODEL, tf), lambda n, f, te, tb, tv: (te[n], 0, fcol(n, f, tv) + nf)),
                  pl.BlockSpec((1, tf, D_MODEL), lambda n, f, te, tb, tv: (te[n], fcol(n, f, tv), 0))],
        out_specs=rows,
        scratch_shapes=[pltpu.VMEM((tm, D_MODEL), BF16), pltpu.VMEM((tm, D_MODEL), F32)],
    )
    return pl.pallas_call(
        _experts_kernel,
        grid_spec=grid_spec,
        out_shape=jax.ShapeDtypeStruct((n_rows, dp), U32),
        compiler_params=pltpu.CompilerParams(
            dimension_semantics=("arbitrary", "arbitrary"), vmem_limit_bytes=VMEM_LIMIT),
        name="experts",
    )(tile_e, tile_blk, tile_valid, xs, w_up, w_up, w_down)


def _combine_kernel(pos_ref, wts_ref, x_ref, gf_ref, ys_ref, o_ref, y1_ref, y2_ref, sem):
    tm = x_ref.shape[0]

    def issue(r, carry):
        _row_copy(ys_ref, pos_ref[r], y1_ref, r, sem).start(priority=0)
        _row_copy(ys_ref, pos_ref[tm + r], y2_ref, r, sem).start(priority=1)
        return carry

    lax.fori_loop(0, tm, issue, 0, unroll=8)
    for buf in (y1_ref, y2_ref):
        pltpu.make_async_copy(ys_ref.at[pl.ds(0, tm), :], buf, sem).wait()

    w1 = wts_ref[:, 0:1]
    w2 = wts_ref[:, 1:2]
    lo1, hi1 = _unpack_bf16_pairs(y1_ref[...])
    lo2, hi2 = _unpack_bf16_pairs(y2_ref[...])
    moe = jnp.concatenate([w1 * lo1 + w2 * lo2, w1 * hi1 + w2 * hi2], axis=-1)
    o_ref[...] = _rmsnorm(x_ref[...] + moe, gf_ref[...])


def _combine(pos_flat, wts, x2, g_final, ys):
    T, D = x2.shape
    tm = TM_ROUTE
    return pl.pallas_call(
        _combine_kernel,
        grid=(T // tm,),
        in_specs=[pl.BlockSpec((TOP_K * tm,), lambda i: (i,), memory_space=pltpu.SMEM),
                  pl.BlockSpec((tm, LANES), lambda i: (i, 0)),
                  pl.BlockSpec((tm, D), lambda i: (i, 0)),
                  _const_spec(g_final.shape),
                  pl.BlockSpec(memory_space=pl.ANY)],
        out_specs=pl.BlockSpec((tm, D), lambda i: (i, 0)),
        out_shape=jax.ShapeDtypeStruct((T, D), F32),
        scratch_shapes=[pltpu.VMEM((tm, D_PACK), U32), pltpu.VMEM((tm, D_PACK), U32),
                        pltpu.SemaphoreType.DMA(())],
        compiler_params=pltpu.CompilerParams(
            dimension_semantics=("arbitrary",), vmem_limit_bytes=VMEM_LIMIT),
        name="combine",
    )(pos_flat, wts, x2, g_final, ys)


def _tile_plan(counts, cap, tm, nt):
    tiles_e = (counts + tm - 1) // tm
    ends = jnp.cumsum(tiles_e)
    total = ends[-1]
    n = jnp.minimum(jnp.arange(nt, dtype=jnp.int32), total - 1)
    e = jnp.minimum(jnp.searchsorted(ends, n, side="right").astype(jnp.int32), N_EXPERTS - 1)
    local = n - (ends[e] - tiles_e[e])
    blk = e * (cap // tm) + local
    valid = jnp.where(jnp.arange(nt) < total, jnp.clip(counts[e] - local * tm, 0, tm), 0)
    return e, blk.astype(jnp.int32), valid.astype(jnp.int32)


def _moe_layer(x2, g_ffn, w_router, w_up, w_down, g_final):
    T, D = x2.shape
    cap = T
    wr_pad = jnp.pad(w_router, ((0, 0), (0, LANES - N_EXPERTS)))
    hp, wts, pos, cnt = _router(x2, g_ffn, wr_pad, cap)
    pos_flat = pos[:, :TOP_K].reshape(T // TM_ROUTE, TM_ROUTE, TOP_K).transpose(0, 2, 1).reshape(-1)
    xs = _dispatch(pos_flat, hp, N_EXPERTS * cap)
    nt = TOP_K * T // TM_MOE + N_EXPERTS
    tile_e, tile_blk, tile_valid = _tile_plan(cnt[0, :N_EXPERTS], cap, TM_MOE, nt)
    ys = _experts(tile_e, tile_blk, tile_valid, xs, w_up, w_down)
    return _combine(pos_flat, wts, x2, g_final, ys)


def kernel(x, norm_mix, w_in, b_in, conv_w, conv_b, conv_ln_g, conv_ln_b, conv_out, pool_mix, pool_scale,
           pool_out, sgu_ln_g, sgu_ln_b, sgu_w, sgu_b, sgu_out, w_gate, b_gate, w_o, norm_ffn, ffn_w_up,
           ffn_w_down, moe_router, moe_w_up, moe_w_down, norm_final):
    B, S, D = x.shape
    row = lambda a: a.reshape(1, -1)
    c_conv = 2 * CONV_WIDTH

    def pieces(a):
        return a.reshape(a.shape[0], -1, MXU_COLS).transpose(1, 0, 2)

    for i in range(DEPTH):
        params = {
            "norm": row(norm_mix[i]),
            "w_conv_in": w_in[i, :, :c_conv].astype(BF16), "b_conv_in": row(b_in[i, :c_conv]),
            "w_zg": pieces(jnp.concatenate([w_in[i, :, c_conv:], w_gate[i]], axis=1).astype(BF16)),
            "b_zg": pieces(jnp.concatenate([b_in[i, c_conv:], b_gate[i]])[None, :]),
            "conv_w": jnp.broadcast_to(conv_w[i][:, None, :], (CONV_KERNEL, SUBLANES, CONV_WIDTH)),
            "conv_b": row(conv_b[i]),
            "conv_ln_g": row(conv_ln_g[i]), "conv_ln_b": row(conv_ln_b[i]),
            "conv_out": conv_out[i].astype(BF16),
            "pool_mix": pool_mix[i].astype(BF16), "pool_scale": row(pool_scale[i]),
            "pool_out": pool_out[i].astype(BF16),
            "sgu_ln_g": row(sgu_ln_g[i]), "sgu_ln_b": row(sgu_ln_b[i]), "sgu_w": sgu_w[i],
            "sgu_b": jnp.repeat(sgu_b[i].T, SGU_HEAD_DIM, axis=1),
            "sgu_out": sgu_out[i].astype(BF16),
            "w_o": w_o[i].astype(BF16),
        }
        x = _mixer_layer(x, params)
        x2 = x.reshape(B * S, D)
        if i % 2 == 0:
            x2 = _dense_ffn(x2, row(norm_ffn[i]), ffn_w_up[i // 2].astype(BF16),
                            ffn_w_down[i // 2].astype(BF16))
        else:
            assert i == DEPTH - 1
            x2 = _moe_layer(x2, row(norm_ffn[i]), moe_router[i // 2], moe_w_up[i // 2].astype(BF16),
                            moe_w_down[i // 2].astype(BF16), row(norm_final))
        x = x2.reshape(B, S, D)
    return x
```

```python
import functools

import jax
import jax.numpy as jnp
from jax import lax
from jax.experimental import pallas as pl
from jax.experimental.pallas import tpu as pltpu

D_MODEL = 1024
DEPTH = 2
CONV_WIDTH = 512
CONV_KERNEL = 31
POOL_WINDOWS = (2, 4, 8, 16)
POOL_GROUP = 128
POOL_WIDTH = POOL_GROUP * len(POOL_WINDOWS)
SGU_HEADS = 4
SGU_HEAD_DIM = 128
SGU_WIDTH = SGU_HEADS * SGU_HEAD_DIM
SGU_BLOCK = 128
N_BRANCH = 3
IN_WIDTH = 2 * CONV_WIDTH + POOL_WIDTH + 2 * SGU_WIDTH
D_FF = 2816
N_EXPERTS = 8
TOP_K = 2
D_FF_EXPERT = 3584
EPS = 1e-6

LANES = 128
SUBLANES = 8
MXU_COLS = 256
N_Z_PIECES = (POOL_WIDTH + 2 * SGU_WIDTH) // MXU_COLS
CONV_HALO = 32
POOL_HALO = 16
VMEM_LIMIT = 56 * 1024 * 1024

TM_MIX = 512
CONV_ROWS = 32
TM_FFN = 512
TM_ROUTE = 512
TM_MOE = 1024
TF_MOE = 512

F32 = jnp.float32
BF16 = jnp.bfloat16


def _const_spec(shape):
    zeros = (0,) * len(shape)
    return pl.BlockSpec(shape, lambda *_: zeros, pipeline_mode=pl.Buffered(1))


def _rmsnorm(x, g):
    return x * lax.rsqrt(jnp.mean(x * x, axis=-1, keepdims=True) + EPS) * g


def _layernorm(x, g, b):
    mu = jnp.mean(x, axis=-1, keepdims=True)
    xc = x - mu
    var = jnp.mean(xc * xc, axis=-1, keepdims=True)
    return xc * lax.rsqrt(var + EPS) * g + b


def _dot(a, b):
    return jnp.dot(a, b, preferred_element_type=F32)


def _dot_split(a, b):
    a_hi = a.astype(BF16)
    a_lo = (a - a_hi.astype(F32)).astype(BF16)
    b_hi = b.astype(BF16)
    b_lo = (b - b_hi.astype(F32)).astype(BF16)
    return _dot(a_hi, b_hi) + (_dot(a_hi, b_lo) + _dot(a_lo, b_hi))


def _mixer_kernel(x_ref, g_ref, wc_ref, bc_ref, wzg_ref, bzg_ref, cw_ref, cb_ref, clg_ref, clb_ref, cout_ref,
                  pmix_ref, pscale_ref, pout_ref, slg_ref, slb_ref, sw_ref, sb_ref, sout_ref, wo_ref,
                  o_ref, hb_buf, conv_ext, conv_sh, pool_ext, zg_buf, act_buf, m_buf):
    tm = x_ref.shape[1]
    j = pl.program_id(1)
    x = x_ref[0]
    hb_buf[...] = _rmsnorm(x, g_ref[...]).astype(BF16)

    def proj(lo, hi):
        return _dot(hb_buf[...], wc_ref[:, lo:hi]) + bc_ref[:, lo:hi]

    @pl.when(j == 0)
    def _():
        conv_ext[0:CONV_HALO, :] = jnp.zeros((CONV_HALO, CONV_WIDTH), F32)
        pool_ext[0:POOL_HALO, :] = jnp.zeros((POOL_HALO, POOL_WIDTH), F32)

    @pl.when(j > 0)
    def _():
        conv_ext[0:CONV_HALO, :] = conv_ext[tm:tm + CONV_HALO, :]
        pool_ext[0:POOL_HALO, :] = pool_ext[tm:tm + POOL_HALO, :]

    conv_ext[CONV_HALO:, :] = proj(0, CONV_WIDTH) * jax.nn.sigmoid(proj(CONV_WIDTH, 2 * CONV_WIDTH))

    n_groups = conv_sh.shape[1]
    for m in range(SUBLANES):
        groups = n_groups - (1 if m else 0)
        conv_sh[m, 0:groups] = conv_ext[m:m + groups * SUBLANES, :].reshape(groups, SUBLANES, CONV_WIDTH)

    base = CONV_HALO - (CONV_KERNEL - 1)
    gpc = CONV_ROWS // SUBLANES

    def conv_chunk(ci):
        acc = jnp.zeros((gpc, SUBLANES, CONV_WIDTH), F32) + cb_ref[...]
        for k in range(CONV_KERNEL):
            q, m = divmod(base + k, SUBLANES)
            acc = acc + conv_sh[m, pl.ds(ci * gpc + q, gpc)] * cw_ref[k]
        hn = _layernorm(acc.reshape(CONV_ROWS, CONV_WIDTH), clg_ref[...], clb_ref[...])
        rows = pl.ds(pl.multiple_of(ci * CONV_ROWS, CONV_ROWS), CONV_ROWS)
        act_buf[rows, :] = (hn * jax.nn.sigmoid(hn)).astype(BF16)

    def piece(c, act):
        zg_buf[c] = act(_dot(hb_buf[...], wzg_ref[c]) + bzg_ref[c])

    n_chunks = tm // CONV_ROWS
    n_pieces = zg_buf.shape[0]
    assert n_chunks <= n_pieces
    for c in range(n_pieces):
        piece(c, (lambda z: z) if c < N_Z_PIECES else jax.nn.sigmoid)
        if c >= n_pieces - n_chunks:
            conv_chunk(c - (n_pieces - n_chunks))

    def z_cols(lo, hi):
        return jnp.concatenate([zg_buf[c] for c in range(lo // MXU_COLS, hi // MXU_COLS)], axis=-1)

    def gate(k):
        c0 = N_Z_PIECES + k * (D_MODEL // MXU_COLS)
        return jnp.concatenate([zg_buf[c] for c in range(c0, c0 + D_MODEL // MXU_COLS)], axis=-1)

    p = z_cols(0, POOL_WIDTH)
    pool_ext[POOL_HALO:, :] = p
    head_pos = lax.broadcasted_iota(jnp.int32, (POOL_HALO, POOL_GROUP), 0)
    diffs = []
    for gi, w in enumerate(POOL_WINDOWS):
        lo, hi = gi * POOL_GROUP, (gi + 1) * POOL_GROUP
        s = pool_ext[:, lo:hi]
        step = 1
        while step < w:
            s = s + pltpu.roll(s, step, axis=0)
            step *= 2
        s = s[POOL_HALO:, :]
        mean = s * (1.0 / w)
        head_cnt = jnp.minimum(head_pos + 1, w).astype(F32)
        head = jnp.where(j == 0, s[:POOL_HALO, :] / head_cnt, mean[:POOL_HALO, :])
        mean = jnp.concatenate([head, mean[POOL_HALO:, :]], axis=0)
        diffs.append((mean - p[:, lo:hi]).astype(BF16))
    u = jax.nn.gelu(z_cols(POOL_WIDTH, POOL_WIDTH + SGU_WIDTH), approximate=True)
    mixed = [_dot(diffs[gi], pmix_ref[gi]) for gi in range(len(POOL_WINDOWS))]
    yb = jnp.concatenate(mixed, axis=-1) * pscale_ref[...]
    v = jax.nn.gelu(z_cols(POOL_WIDTH + SGU_WIDTH, POOL_WIDTH + 2 * SGU_WIDTH), approximate=True)
    vn = _layernorm(v, slg_ref[...], slb_ref[...]).astype(BF16)
    m_buf[...] = gate(1) * _dot(yb.astype(BF16), pout_ref[...])

    row = lax.broadcasted_iota(jnp.int32, (SGU_BLOCK, SGU_BLOCK), 0)
    col = lax.broadcasted_iota(jnp.int32, (SGU_BLOCK, SGU_BLOCK), 1)
    ws = [jnp.where(row >= col, sw_ref[hd], 0.0).astype(BF16) for hd in range(SGU_HEADS)]
    blocks = []
    for r in range(0, tm, SGU_BLOCK):
        heads = [_dot(ws[hd], vn[r:r + SGU_BLOCK, hd * SGU_HEAD_DIM:(hd + 1) * SGU_HEAD_DIM])
                 for hd in range(SGU_HEADS)]
        blocks.append(jnp.concatenate(heads, axis=-1) + sb_ref[...])
    y = u * jnp.concatenate(blocks, axis=0)
    m_buf[...] += gate(2) * _dot(y.astype(BF16), sout_ref[...])
    merged = m_buf[...] + gate(0) * _dot(act_buf[...], cout_ref[...])

    o_ref[0] = x + _dot(merged.astype(BF16), wo_ref[...])


def _mixer_layer(x, p):
    B, S, D = x.shape
    tm = TM_MIX
    consts = [p["norm"], p["w_conv_in"], p["b_conv_in"], p["w_zg"], p["b_zg"], p["conv_w"], p["conv_b"],
              p["conv_ln_g"], p["conv_ln_b"], p["conv_out"], p["pool_mix"], p["pool_scale"], p["pool_out"],
              p["sgu_ln_g"], p["sgu_ln_b"], p["sgu_w"], p["sgu_b"], p["sgu_out"], p["w_o"]]
    x_spec = pl.BlockSpec((1, tm, D), lambda b, j: (b, j, 0))
    n_pieces = p["w_zg"].shape[0]
    return pl.pallas_call(
        _mixer_kernel,
        grid=(B, S // tm),
        in_specs=[x_spec] + [_const_spec(c.shape) for c in consts],
        out_specs=x_spec,
        out_shape=jax.ShapeDtypeStruct(x.shape, F32),
        scratch_shapes=[pltpu.VMEM((tm, D), BF16),
                        pltpu.VMEM((tm + CONV_HALO, CONV_WIDTH), F32),
                        pltpu.VMEM((SUBLANES, (tm + CONV_HALO) // SUBLANES, SUBLANES, CONV_WIDTH), F32),
                        pltpu.VMEM((tm + POOL_HALO, POOL_WIDTH), F32),
                        pltpu.VMEM((n_pieces, tm, MXU_COLS), F32),
                        pltpu.VMEM((tm, CONV_WIDTH), BF16),
                        pltpu.VMEM((tm, D), F32)],
        compiler_params=pltpu.CompilerParams(
            dimension_semantics=("arbitrary", "arbitrary"), vmem_limit_bytes=VMEM_LIMIT),
        name="mixer",
    )(x, *consts)


def _ffn_kernel(x_ref, g_ref, wup_ref, wdn_ref, o_ref):
    x = x_ref[...]
    hb = _rmsnorm(x, g_ref[...]).astype(BF16)
    a = _dot(hb, wup_ref[:, :D_FF])
    b = _dot(hb, wup_ref[:, D_FF:])
    act = (a * jax.nn.sigmoid(a) * b).astype(BF16)
    o_ref[...] = x + _dot(act, wdn_ref[...])


def _dense_ffn(x2, g, w_up, w_down):
    T, D = x2.shape
    tm = TM_FFN
    x_spec = pl.BlockSpec((tm, D), lambda i: (i, 0))
    return pl.pallas_call(
        _ffn_kernel,
        grid=(T // tm,),
        in_specs=[x_spec, _const_spec(g.shape), _const_spec(w_up.shape), _const_spec(w_down.shape)],
        out_specs=x_spec,
        out_shape=jax.ShapeDtypeStruct(x2.shape, F32),
        compiler_params=pltpu.CompilerParams(
            dimension_semantics=("arbitrary",), vmem_limit_bytes=VMEM_LIMIT),
        name="dense_ffn",
    )(x2, g, w_up, w_down)


def _router_kernel(x_ref, g_ref, wr_ref, hp_ref, wts_ref, pos_ref, cnt_ref, carry_ref, *, cap):
    tm = x_ref.shape[0]

    @pl.when(pl.program_id(0) == 0)
    def _():
        carry_ref[...] = jnp.zeros_like(carry_ref)

    h = _rmsnorm(x_ref[...], g_ref[...])
    hp_ref[...] = h
    logits = _dot_split(h, wr_ref[...])
    lane = lax.broadcasted_iota(jnp.int32, logits.shape, 1)
    neg = jnp.float32(-jnp.inf)
    logits = jnp.where(lane < N_EXPERTS, logits, neg)
    m1 = jnp.max(logits, axis=-1, keepdims=True)
    i1 = jnp.min(jnp.where(logits == m1, lane, LANES), axis=-1, keepdims=True)
    rest = jnp.where(lane == i1, neg, logits)
    m2 = jnp.max(rest, axis=-1, keepdims=True)
    i2 = jnp.min(jnp.where(rest == m2, lane, LANES), axis=-1, keepdims=True)
    e2 = jnp.exp(m2 - m1)
    w1 = 1.0 / (1.0 + e2)
    w2 = e2 / (1.0 + e2)

    chosen = jnp.where(jnp.logical_or(lane == i1, lane == i2), 1.0, 0.0)
    r_io = lax.broadcasted_iota(jnp.int32, (tm, tm), 0)
    c_io = lax.broadcasted_iota(jnp.int32, (tm, tm), 1)
    tri = jnp.where(r_io >= c_io, 1.0, 0.0).astype(BF16)
    incl = _dot(tri, chosen.astype(BF16))
    before = incl - chosen + carry_ref[...]
    rank1 = jnp.sum(jnp.where(lane == i1, before, 0.0), axis=-1, keepdims=True).astype(jnp.int32)
    rank2 = jnp.sum(jnp.where(lane == i2, before, 0.0), axis=-1, keepdims=True).astype(jnp.int32)
    carry_ref[...] = carry_ref[...] + incl[tm - 1:tm, :]
    cnt_ref[...] = carry_ref[...].astype(jnp.int32)

    wts_ref[...] = jnp.where(lane == 0, w1, jnp.where(lane == 1, w2, 0.0))
    pos_ref[...] = jnp.where(lane == 0, i1 * cap + rank1, jnp.where(lane == 1, i2 * cap + rank2, 0))


def _router(x2, g, wr_pad, cap):
    T, D = x2.shape
    tm = TM_ROUTE
    tile = lambda w: pl.BlockSpec((tm, w), lambda i: (i, 0))
    return pl.pallas_call(
        functools.partial(_router_kernel, cap=cap),
        grid=(T // tm,),
        in_specs=[tile(D), _const_spec(g.shape), _const_spec(wr_pad.shape)],
        out_specs=[tile(D), tile(LANES), tile(LANES), pl.BlockSpec((1, LANES), lambda i: (0, 0))],
        out_shape=[jax.ShapeDtypeStruct((T, D), F32), jax.ShapeDtypeStruct((T, LANES), F32),
                   jax.ShapeDtypeStruct((T, LANES), jnp.int32), jax.ShapeDtypeStruct((1, LANES), jnp.int32)],
        scratch_shapes=[pltpu.VMEM((1, LANES), F32)],
        compiler_params=pltpu.CompilerParams(
            dimension_semantics=("arbitrary",), vmem_limit_bytes=VMEM_LIMIT),
        name="router",
    )(x2, g, wr_pad)


def _row_copy(src, src_row, dst, dst_row, sem):
    return pltpu.make_async_copy(src.at[pl.ds(src_row, 1), :], dst.at[pl.ds(dst_row, 1), :], sem)


def _dispatch_kernel(pos_ref, hp_ref, xs_ref, sem):
    tm = hp_ref.shape[0]

    def issue(r, carry):
        _row_copy(hp_ref, r, xs_ref, pos_ref[r], sem).start(priority=0)
        _row_copy(hp_ref, r, xs_ref, pos_ref[tm + r], sem).start(priority=1)
        return carry

    lax.fori_loop(0, tm, issue, 0, unroll=8)
    for _ in range(TOP_K):
        pltpu.make_async_copy(hp_ref, xs_ref.at[pl.ds(0, tm), :], sem).wait()


def _dispatch(pos_flat, hp, n_rows):
    T, dp = hp.shape
    tm = TM_ROUTE
    return pl.pallas_call(
        _dispatch_kernel,
        grid=(T // tm,),
        in_specs=[pl.BlockSpec((TOP_K * tm,), lambda i: (i,), memory_space=pltpu.SMEM),
                  pl.BlockSpec((tm, dp), lambda i: (i, 0))],
        out_specs=pl.BlockSpec(memory_space=pl.ANY),
        out_shape=jax.ShapeDtypeStruct((n_rows, dp), F32),
        scratch_shapes=[pltpu.SemaphoreType.DMA(())],
        compiler_params=pltpu.CompilerParams(
            dimension_semantics=("arbitrary",), vmem_limit_bytes=VMEM_LIMIT, has_side_effects=True),
        name="dispatch",
    )(pos_flat, hp)


def _experts_kernel(te_ref, tb_ref, tv_ref, xs_ref, wa_ref, wb_ref, wd_ref, ys_ref, xb_ref):
    n, f = pl.program_id(0), pl.program_id(1)
    tm = xs_ref.shape[0]
    valid = tv_ref[n]

    @pl.when(valid > 0)
    def _():
        @pl.when(f == 0)
        def _():
            keep = lax.broadcasted_iota(jnp.int32, xs_ref.shape, 0) < valid
            xb_ref[...] = jnp.where(keep, xs_ref[...], 0.0).astype(BF16)

        xb = xb_ref[...]
        a = _dot(xb, wa_ref[0].astype(BF16))
        b = _dot(xb, wb_ref[0].astype(BF16))
        act = (a * jax.nn.sigmoid(a) * b).astype(BF16)
        part = _dot(act, wd_ref[0].astype(BF16))

        @pl.when(f == 0)
        def _():
            ys_ref[...] = part

        @pl.when(f > 0)
        def _():
            ys_ref[...] += part


def _experts(tile_e, tile_blk, tile_valid, xs, w_up, w_down):
    n_rows, dp = xs.shape
    tm, tf = TM_MOE, TF_MOE
    nf = D_FF_EXPERT // tf
    nt = tile_e.shape[0]

    def fcol(n, f, tv):
        return jnp.where(tv[n] > 0, f, nf - 1)

    rows = pl.BlockSpec((tm, dp), lambda n, f, te, tb, tv: (tb[n], 0))
    grid_spec = pltpu.PrefetchScalarGridSpec(
        num_scalar_prefetch=3,
        grid=(nt, nf),
        in_specs=[rows,
                  pl.BlockSpec((1, D_MODEL, tf), lambda n, f, te, tb, tv: (te[n], 0, fcol(n, f, tv))),
                  pl.BlockSpec((1, D_MODEL, tf), lambda n, f, te, tb, tv: (te[n], 0, fcol(n, f, tv) + nf)),
                  pl.BlockSpec((1, tf, D_MODEL), lambda n, f, te, tb, tv: (te[n], fcol(n, f, tv), 0))],
        out_specs=rows,
        scratch_shapes=[pltpu.VMEM((tm, D_MODEL), BF16)],
    )
    return pl.pallas_call(
        _experts_kernel,
        grid_spec=grid_spec,
        out_shape=jax.ShapeDtypeStruct((n_rows, dp), F32),
        compiler_params=pltpu.CompilerParams(
            dimension_semantics=("arbitrary", "arbitrary"), vmem_limit_bytes=VMEM_LIMIT),
        name="experts",
    )(tile_e, tile_blk, tile_valid, xs, w_up, w_up, w_down)


def _combine_kernel(pos_ref, wts_ref, x_ref, gf_ref, ys_ref, o_ref, y1_ref, y2_ref, sem):
    tm = x_ref.shape[0]

    def issue(r, carry):
        _row_copy(ys_ref, pos_ref[r], y1_ref, r, sem).start(priority=0)
        _row_copy(ys_ref, pos_ref[tm + r], y2_ref, r, sem).start(priority=1)
        return carry

    lax.fori_loop(0, tm, issue, 0, unroll=8)
    for buf in (y1_ref, y2_ref):
        pltpu.make_async_copy(ys_ref.at[pl.ds(0, tm), :], buf, sem).wait()

    moe = wts_ref[:, 0:1] * y1_ref[...] + wts_ref[:, 1:2] * y2_ref[...]
    o_ref[...] = _rmsnorm(x_ref[...] + moe, gf_ref[...])


def _combine(pos_flat, wts, x2, g_final, ys):
    T, D = x2.shape
    tm = TM_ROUTE
    return pl.pallas_call(
        _combine_kernel,
        grid=(T // tm,),
        in_specs=[pl.BlockSpec((TOP_K * tm,), lambda i: (i,), memory_space=pltpu.SMEM),
                  pl.BlockSpec((tm, LANES), lambda i: (i, 0)),
                  pl.BlockSpec((tm, D), lambda i: (i, 0)),
                  _const_spec(g_final.shape),
                  pl.BlockSpec(memory_space=pl.ANY)],
        out_specs=pl.BlockSpec((tm, D), lambda i: (i, 0)),
        out_shape=jax.ShapeDtypeStruct((T, D), F32),
        scratch_shapes=[pltpu.VMEM((tm, D), F32), pltpu.VMEM((tm, D), F32), pltpu.SemaphoreType.DMA(())],
        compiler_params=pltpu.CompilerParams(
            dimension_semantics=("arbitrary",), vmem_limit_bytes=VMEM_LIMIT),
        name="combine",
    )(pos_flat, wts, x2, g_final, ys)


def _tile_plan(counts, cap, tm, nt):
    tiles_e = (counts + tm - 1) // tm
    ends = jnp.cumsum(tiles_e)
    total = ends[-1]
    n = jnp.minimum(jnp.arange(nt, dtype=jnp.int32), total - 1)
    e = jnp.minimum(jnp.sum(n[:, None] >= ends[None, :], axis=1).astype(jnp.int32), N_EXPERTS - 1)
    local = n - (ends[e] - tiles_e[e])
    blk = e * (cap // tm) + local
    valid = jnp.where(jnp.arange(nt) < total, jnp.clip(counts[e] - local * tm, 0, tm), 0)
    return e, blk.astype(jnp.int32), valid.astype(jnp.int32)


def _moe_layer(x2, g_ffn, w_router, w_up, w_down, g_final):
    T, D = x2.shape
    cap = T
    wr_pad = jnp.pad(w_router, ((0, 0), (0, LANES - N_EXPERTS)))
    hp, wts, pos, cnt = _router(x2, g_ffn, wr_pad, cap)
    pos_flat = pos[:, :TOP_K].reshape(T // TM_ROUTE, TM_ROUTE, TOP_K).transpose(0, 2, 1).reshape(-1)
    xs = _dispatch(pos_flat, hp, N_EXPERTS * cap)
    nt = TOP_K * T // TM_MOE + N_EXPERTS
    tile_e, tile_blk, tile_valid = _tile_plan(cnt[0, :N_EXPERTS], cap, TM_MOE, nt)
    ys = _experts(tile_e, tile_blk, tile_valid, xs, w_up, w_down)
    return _combine(pos_flat, wts, x2, g_final, ys)


def kernel(x, norm_mix, w_in, b_in, conv_w, conv_b, conv_ln_g, conv_ln_b, conv_out, pool_mix, pool_scale,
           pool_out, sgu_ln_g, sgu_ln_b, sgu_w, sgu_b, sgu_out, w_gate, b_gate, w_o, norm_ffn, ffn_w_up,
           ffn_w_down, moe_router, moe_w_up, moe_w_down, norm_final):
    B, S, D = x.shape
    row = lambda a: a.reshape(1, -1)
    c_conv = 2 * CONV_WIDTH

    def pieces(a):
        return a.reshape(a.shape[0], -1, MXU_COLS).transpose(1, 0, 2)

    for i in range(DEPTH):
        params = {
            "norm": row(norm_mix[i]),
            "w_conv_in": w_in[i, :, :c_conv].astype(BF16), "b_conv_in": row(b_in[i, :c_conv]),
            "w_zg": pieces(jnp.concatenate([w_in[i, :, c_conv:], w_gate[i]], axis=1).astype(BF16)),
            "b_zg": pieces(jnp.concatenate([b_in[i, c_conv:], b_gate[i]])[None, :]),
            "conv_w": jnp.broadcast_to(conv_w[i][:, None, :], (CONV_KERNEL, SUBLANES, CONV_WIDTH)),
            "conv_b": row(conv_b[i]),
            "conv_ln_g": row(conv_ln_g[i]), "conv_ln_b": row(conv_ln_b[i]),
            "conv_out": conv_out[i].astype(BF16),
            "pool_mix": pool_mix[i].astype(BF16), "pool_scale": row(pool_scale[i]),
            "pool_out": pool_out[i].astype(BF16),
            "sgu_ln_g": row(sgu_ln_g[i]), "sgu_ln_b": row(sgu_ln_b[i]), "sgu_w": sgu_w[i],
            "sgu_b": jnp.repeat(sgu_b[i].T, SGU_HEAD_DIM, axis=1),
            "sgu_out": sgu_out[i].astype(BF16),
            "w_o": w_o[i].astype(BF16),
        }
        x = _mixer_layer(x, params)
        x2 = x.reshape(B * S, D)
        if i % 2 == 0:
            x2 = _dense_ffn(x2, row(norm_ffn[i]), ffn_w_up[i // 2].astype(BF16),
                            ffn_w_down[i // 2].astype(BF16))
        else:
            assert i == DEPTH - 1
            x2 = _moe_layer(x2, row(norm_ffn[i]), moe_router[i // 2], moe_w_up[i // 2], moe_w_down[i // 2],
                            row(norm_final))
        x = x2.reshape(B, S, D)
    return x
```

```python
import functools

import jax
import jax.numpy as jnp
from jax import lax
from jax.experimental import pallas as pl
from jax.experimental.pallas import tpu as pltpu

D_MODEL = 1024
DEPTH = 2
CONV_WIDTH = 512
CONV_KERNEL = 31
POOL_WINDOWS = (2, 4, 8, 16)
POOL_GROUP = 128
POOL_WIDTH = POOL_GROUP * len(POOL_WINDOWS)
SGU_HEADS = 4
SGU_HEAD_DIM = 128
SGU_WIDTH = SGU_HEADS * SGU_HEAD_DIM
SGU_BLOCK = 128
N_BRANCH = 3
IN_WIDTH = 2 * CONV_WIDTH + POOL_WIDTH + 2 * SGU_WIDTH
D_FF = 2816
N_EXPERTS = 8
TOP_K = 2
D_FF_EXPERT = 3584
EPS = 1e-6

LANES = 128
SUBLANES = 8
MXU_COLS = 256
N_Z_PIECES = (POOL_WIDTH + 2 * SGU_WIDTH) // MXU_COLS
CONV_HALO = 32
POOL_HALO = 16
VMEM_LIMIT = 56 * 1024 * 1024

TM_MIX = 512
CONV_ROWS = 32
TM_FFN = 512
TM_ROUTE = 512
TM_MOE = 1024
TF_MOE = 512

F32 = jnp.float32
BF16 = jnp.bfloat16


def _const_spec(shape):
    zeros = (0,) * len(shape)
    return pl.BlockSpec(shape, lambda *_: zeros, pipeline_mode=pl.Buffered(1))


def _rmsnorm(x, g):
    return x * lax.rsqrt(jnp.mean(x * x, axis=-1, keepdims=True) + EPS) * g


def _layernorm(x, g, b):
    mu = jnp.mean(x, axis=-1, keepdims=True)
    xc = x - mu
    var = jnp.mean(xc * xc, axis=-1, keepdims=True)
    return xc * lax.rsqrt(var + EPS) * g + b


def _dot(a, b):
    return jnp.dot(a, b, preferred_element_type=F32)


def _dot_split(a, b):
    a_hi = a.astype(BF16)
    a_lo = (a - a_hi.astype(F32)).astype(BF16)
    b_hi = b.astype(BF16)
    b_lo = (b - b_hi.astype(F32)).astype(BF16)
    return _dot(a_hi, b_hi) + (_dot(a_hi, b_lo) + _dot(a_lo, b_hi))


def _mixer_kernel(x_ref, g_ref, wc_ref, bc_ref, wzg_ref, bzg_ref, cw_ref, cb_ref, clg_ref, clb_ref, cout_ref,
                  pmix_ref, pscale_ref, pout_ref, slg_ref, slb_ref, sw_ref, sb_ref, sout_ref, wo_ref,
                  o_ref, hb_buf, conv_ext, conv_sh, pool_ext, zg_buf, act_buf, m_buf):
    tm = x_ref.shape[1]
    j = pl.program_id(1)
    x = x_ref[0]
    hb_buf[...] = _rmsnorm(x, g_ref[...]).astype(BF16)

    def proj(lo, hi):
        return _dot(hb_buf[...], wc_ref[:, lo:hi]) + bc_ref[:, lo:hi]

    @pl.when(j == 0)
    def _():
        conv_ext[0:CONV_HALO, :] = jnp.zeros((CONV_HALO, CONV_WIDTH), F32)
        pool_ext[0:POOL_HALO, :] = jnp.zeros((POOL_HALO, POOL_WIDTH), F32)

    @pl.when(j > 0)
    def _():
        conv_ext[0:CONV_HALO, :] = conv_ext[tm:tm + CONV_HALO, :]
        pool_ext[0:POOL_HALO, :] = pool_ext[tm:tm + POOL_HALO, :]

    conv_ext[CONV_HALO:, :] = proj(0, CONV_WIDTH) * jax.nn.sigmoid(proj(CONV_WIDTH, 2 * CONV_WIDTH))

    n_groups = conv_sh.shape[1]
    for m in range(SUBLANES):
        groups = n_groups - (1 if m else 0)
        conv_sh[m, 0:groups] = conv_ext[m:m + groups * SUBLANES, :].reshape(groups, SUBLANES, CONV_WIDTH)

    base = CONV_HALO - (CONV_KERNEL - 1)
    gpc = CONV_ROWS // SUBLANES

    def conv_chunk(ci):
        acc = jnp.zeros((gpc, SUBLANES, CONV_WIDTH), F32) + cb_ref[...]
        for k in range(CONV_KERNEL):
            q, m = divmod(base + k, SUBLANES)
            acc = acc + conv_sh[m, pl.ds(ci * gpc + q, gpc)] * cw_ref[k]
        hn = _layernorm(acc.reshape(CONV_ROWS, CONV_WIDTH), clg_ref[...], clb_ref[...])
        rows = pl.ds(pl.multiple_of(ci * CONV_ROWS, CONV_ROWS), CONV_ROWS)
        act_buf[rows, :] = (hn * jax.nn.sigmoid(hn)).astype(BF16)

    def piece(c, act):
        zg_buf[c] = act(_dot(hb_buf[...], wzg_ref[c]) + bzg_ref[c])

    n_chunks = tm // CONV_ROWS
    n_pieces = zg_buf.shape[0]
    assert n_chunks <= n_pieces
    for c in range(n_pieces):
        piece(c, (lambda z: z) if c < N_Z_PIECES else jax.nn.sigmoid)
        if c >= n_pieces - n_chunks:
            conv_chunk(c - (n_pieces - n_chunks))

    def z_cols(lo, hi):
        return jnp.concatenate([zg_buf[c] for c in range(lo // MXU_COLS, hi // MXU_COLS)], axis=-1)

    def gate(k):
        c0 = N_Z_PIECES + k * (D_MODEL // MXU_COLS)
        return jnp.concatenate([zg_buf[c] for c in range(c0, c0 + D_MODEL // MXU_COLS)], axis=-1)

    p = z_cols(0, POOL_WIDTH)
    pool_ext[POOL_HALO:, :] = p
    head_pos = lax.broadcasted_iota(jnp.int32, (POOL_HALO, POOL_GROUP), 0)
    diffs = []
    for gi, w in enumerate(POOL_WINDOWS):
        lo, hi = gi * POOL_GROUP, (gi + 1) * POOL_GROUP
        s = pool_ext[:, lo:hi]
        step = 1
        while step < w:
            s = s + pltpu.roll(s, step, axis=0)
            step *= 2
        s = s[POOL_HALO:, :]
        mean = s * (1.0 / w)
        head_cnt = jnp.minimum(head_pos + 1, w).astype(F32)
        head = jnp.where(j == 0, s[:POOL_HALO, :] / head_cnt, mean[:POOL_HALO, :])
        mean = jnp.concatenate([head, mean[POOL_HALO:, :]], axis=0)
        diffs.append((mean - p[:, lo:hi]).astype(BF16))
    u = jax.nn.gelu(z_cols(POOL_WIDTH, POOL_WIDTH + SGU_WIDTH), approximate=True)
    mixed = [_dot(diffs[gi], pmix_ref[gi]) for gi in range(len(POOL_WINDOWS))]
    yb = jnp.concatenate(mixed, axis=-1) * pscale_ref[...]
    v = jax.nn.gelu(z_cols(POOL_WIDTH + SGU_WIDTH, POOL_WIDTH + 2 * SGU_WIDTH), approximate=True)
    vn = _layernorm(v, slg_ref[...], slb_ref[...]).astype(BF16)
    m_buf[...] = gate(1) * _dot(yb.astype(BF16), pout_ref[...])

    row = lax.broadcasted_iota(jnp.int32, (SGU_BLOCK, SGU_BLOCK), 0)
    col = lax.broadcasted_iota(jnp.int32, (SGU_BLOCK, SGU_BLOCK), 1)
    ws = [jnp.where(row >= col, sw_ref[hd], 0.0).astype(BF16) for hd in range(SGU_HEADS)]
    blocks = []
    for r in range(0, tm, SGU_BLOCK):
        heads = [_dot(ws[hd], vn[r:r + SGU_BLOCK, hd * SGU_HEAD_DIM:(hd + 1) * SGU_HEAD_DIM])
                 for hd in range(SGU_HEADS)]
        blocks.append(jnp.concatenate(heads, axis=-1) + sb_ref[...])
    y = u * jnp.concatenate(blocks, axis=0)
    m_buf[...] += gate(2) * _dot(y.astype(BF16), sout_ref[...])
    merged = m_buf[...] + gate(0) * _dot(act_buf[...], cout_ref[...])

    o_ref[0] = x + _dot(merged.astype(BF16), wo_ref[...])


def _mixer_layer(x, p):
    B, S, D = x.shape
    tm = TM_MIX
    consts = [p["norm"], p["w_conv_in"], p["b_conv_in"], p["w_zg"], p["b_zg"], p["conv_w"], p["conv_b"],
              p["conv_ln_g"], p["conv_ln_b"], p["conv_out"], p["pool_mix"], p["pool_scale"], p["pool_out"],
              p["sgu_ln_g"], p["sgu_ln_b"], p["sgu_w"], p["sgu_b"], p["sgu_out"], p["w_o"]]
    x_spec = pl.BlockSpec((1, tm, D), lambda b, j: (b, j, 0))
    n_pieces = p["w_zg"].shape[0]
    return pl.pallas_call(
        _mixer_kernel,
        grid=(B, S // tm),
        in_specs=[x_spec] + [_const_spec(c.shape) for c in consts],
        out_specs=x_spec,
        out_shape=jax.ShapeDtypeStruct(x.shape, F32),
        scratch_shapes=[pltpu.VMEM((tm, D), BF16),
                        pltpu.VMEM((tm + CONV_HALO, CONV_WIDTH), F32),
                        pltpu.VMEM((SUBLANES, (tm + CONV_HALO) // SUBLANES, SUBLANES, CONV_WIDTH), F32),
                        pltpu.VMEM((tm + POOL_HALO, POOL_WIDTH), F32),
                        pltpu.VMEM((n_pieces, tm, MXU_COLS), F32),
                        pltpu.VMEM((tm, CONV_WIDTH), BF16),
                        pltpu.VMEM((tm, D), F32)],
        compiler_params=pltpu.CompilerParams(
            dimension_semantics=("arbitrary", "arbitrary"), vmem_limit_bytes=VMEM_LIMIT),
        name="mixer",
    )(x, *consts)


def _ffn_kernel(x_ref, g_ref, wup_ref, wdn_ref, o_ref):
    x = x_ref[...]
    hb = _rmsnorm(x, g_ref[...]).astype(BF16)
    a = _dot(hb, wup_ref[:, :D_FF])
    b = _dot(hb, wup_ref[:, D_FF:])
    act = (a * jax.nn.sigmoid(a) * b).astype(BF16)
    o_ref[...] = x + _dot(act, wdn_ref[...])


def _dense_ffn(x2, g, w_up, w_down):
    T, D = x2.shape
    tm = TM_FFN
    x_spec = pl.BlockSpec((tm, D), lambda i: (i, 0))
    return pl.pallas_call(
        _ffn_kernel,
        grid=(T // tm,),
        in_specs=[x_spec, _const_spec(g.shape), _const_spec(w_up.shape), _const_spec(w_down.shape)],
        out_specs=x_spec,
        out_shape=jax.ShapeDtypeStruct(x2.shape, F32),
        compiler_params=pltpu.CompilerParams(
            dimension_semantics=("arbitrary",), vmem_limit_bytes=VMEM_LIMIT),
        name="dense_ffn",
    )(x2, g, w_up, w_down)


def _router_kernel(x_ref, g_ref, wr_ref, hp_ref, wts_ref, pos_ref, cnt_ref, carry_ref, *, cap):
    tm = x_ref.shape[0]

    @pl.when(pl.program_id(0) == 0)
    def _():
        carry_ref[...] = jnp.zeros_like(carry_ref)

    h = _rmsnorm(x_ref[...], g_ref[...])
    hp_ref[...] = h
    logits = _dot_split(h, wr_ref[...])
    lane = lax.broadcasted_iota(jnp.int32, logits.shape, 1)
    neg = jnp.float32(-jnp.inf)
    logits = jnp.where(lane < N_EXPERTS, logits, neg)
    m1 = jnp.max(logits, axis=-1, keepdims=True)
    i1 = jnp.min(jnp.where(logits == m1, lane, LANES), axis=-1, keepdims=True)
    rest = jnp.where(lane == i1, neg, logits)
    m2 = jnp.max(rest, axis=-1, keepdims=True)
    i2 = jnp.min(jnp.where(rest == m2, lane, LANES), axis=-1, keepdims=True)
    e2 = jnp.exp(m2 - m1)
    w1 = 1.0 / (1.0 + e2)
    w2 = e2 / (1.0 + e2)

    chosen = jnp.where(jnp.logical_or(lane == i1, lane == i2), 1.0, 0.0)
    r_io = lax.broadcasted_iota(jnp.int32, (tm, tm), 0)
    c_io = lax.broadcasted_iota(jnp.int32, (tm, tm), 1)
    tri = jnp.where(r_io >= c_io, 1.0, 0.0).astype(BF16)
    incl = _dot(tri, chosen.astype(BF16))
    before = incl - chosen + carry_ref[...]
    rank1 = jnp.sum(jnp.where(lane == i1, before, 0.0), axis=-1, keepdims=True).astype(jnp.int32)
    rank2 = jnp.sum(jnp.where(lane == i2, before, 0.0), axis=-1, keepdims=True).astype(jnp.int32)
    carry_ref[...] = carry_ref[...] + incl[tm - 1:tm, :]
    cnt_ref[...] = carry_ref[...].astype(jnp.int32)

    wts_ref[...] = jnp.where(lane == 0, w1, jnp.where(lane == 1, w2, 0.0))
    pos_ref[...] = jnp.where(lane == 0, i1 * cap + rank1, jnp.where(lane == 1, i2 * cap + rank2, 0))


def _router(x2, g, wr_pad, cap):
    T, D = x2.shape
    tm = TM_ROUTE
    tile = lambda w: pl.BlockSpec((tm, w), lambda i: (i, 0))
    return pl.pallas_call(
        functools.partial(_router_kernel, cap=cap),
        grid=(T // tm,),
        in_specs=[tile(D), _const_spec(g.shape), _const_spec(wr_pad.shape)],
        out_specs=[tile(D), tile(LANES), tile(LANES), pl.BlockSpec((1, LANES), lambda i: (0, 0))],
        out_shape=[jax.ShapeDtypeStruct((T, D), F32), jax.ShapeDtypeStruct((T, LANES), F32),
                   jax.ShapeDtypeStruct((T, LANES), jnp.int32), jax.ShapeDtypeStruct((1, LANES), jnp.int32)],
        scratch_shapes=[pltpu.VMEM((1, LANES), F32)],
        compiler_params=pltpu.CompilerParams(
            dimension_semantics=("arbitrary",), vmem_limit_bytes=VMEM_LIMIT),
        name="router",
    )(x2, g, wr_pad)


def _dispatch_kernel(pos_ref, hp_ref, xs_ref, sem):
    groups = hp_ref.shape[0]
    tm = groups * SUBLANES

    def issue(g, carry):
        for u in range(SUBLANES):
            r = g * SUBLANES + u
            src = hp_ref.at[g, pl.ds(u, 1), :]
            pltpu.make_async_copy(src, xs_ref.at[pl.ds(pos_ref[r], 1), :], sem).start(priority=0)
            pltpu.make_async_copy(src, xs_ref.at[pl.ds(pos_ref[tm + r], 1), :], sem).start(priority=1)
        return carry

    lax.fori_loop(0, groups, issue, 0)
    for _ in range(TOP_K):
        pltpu.make_async_copy(hp_ref, hp_ref, sem).wait()


def _dispatch(pos_flat, hp, n_rows):
    T, dp = hp.shape
    tm = TM_ROUTE
    return pl.pallas_call(
        _dispatch_kernel,
        grid=(T // tm,),
        in_specs=[pl.BlockSpec((TOP_K * tm,), lambda i: (i,), memory_space=pltpu.SMEM),
                  pl.BlockSpec((tm // SUBLANES, SUBLANES, dp), lambda i: (i, 0, 0))],
        out_specs=pl.BlockSpec(memory_space=pl.ANY),
        out_shape=jax.ShapeDtypeStruct((n_rows, dp), F32),
        scratch_shapes=[pltpu.SemaphoreType.DMA(())],
        compiler_params=pltpu.CompilerParams(
            dimension_semantics=("arbitrary",), vmem_limit_bytes=VMEM_LIMIT, has_side_effects=True),
        name="dispatch",
    )(pos_flat, hp.reshape(T // SUBLANES, SUBLANES, dp))


def _experts_kernel(te_ref, tb_ref, tv_ref, xs_ref, wa_ref, wb_ref, wd_ref, ys_ref, xb_ref):
    n, f = pl.program_id(0), pl.program_id(1)
    tm = xs_ref.shape[0]
    valid = tv_ref[n]

    @pl.when(valid > 0)
    def _():
        @pl.when(f == 0)
        def _():
            keep = lax.broadcasted_iota(jnp.int32, xs_ref.shape, 0) < valid
            xb_ref[...] = jnp.where(keep, xs_ref[...], 0.0).astype(BF16)

        xb = xb_ref[...]
        a = _dot(xb, wa_ref[0].astype(BF16))
        b = _dot(xb, wb_ref[0].astype(BF16))
        act = (a * jax.nn.sigmoid(a) * b).astype(BF16)
        part = _dot(act, wd_ref[0].astype(BF16))

        @pl.when(f == 0)
        def _():
            ys_ref[...] = part

        @pl.when(f > 0)
        def _():
            ys_ref[...] += part


def _experts(tile_e, tile_blk, tile_valid, xs, w_up, w_down):
    n_rows, dp = xs.shape
    tm, tf = TM_MOE, TF_MOE
    nf = D_FF_EXPERT // tf
    nt = tile_e.shape[0]

    def fcol(n, f, tv):
        return jnp.where(tv[n] > 0, f, nf - 1)

    rows = pl.BlockSpec((tm, dp), lambda n, f, te, tb, tv: (tb[n], 0))
    grid_spec = pltpu.PrefetchScalarGridSpec(
        num_scalar_prefetch=3,
        grid=(nt, nf),
        in_specs=[rows,
                  pl.BlockSpec((1, D_MODEL, tf), lambda n, f, te, tb, tv: (te[n], 0, fcol(n, f, tv))),
                  pl.BlockSpec((1, D_MODEL, tf), lambda n, f, te, tb, tv: (te[n], 0, fcol(n, f, tv) + nf)),
                  pl.BlockSpec((1, tf, D_MODEL), lambda n, f, te, tb, tv: (te[n], fcol(n, f, tv), 0))],
        out_specs=rows,
        scratch_shapes=[pltpu.VMEM((tm, D_MODEL), BF16)],
    )
    return pl.pallas_call(
        _experts_kernel,
        grid_spec=grid_spec,
        out_shape=jax.ShapeDtypeStruct((n_rows, dp), F32),
        compiler_params=pltpu.CompilerParams(
            dimension_semantics=("arbitrary", "arbitrary"), vmem_limit_bytes=VMEM_LIMIT),
        name="experts",
    )(tile_e, tile_blk, tile_valid, xs, w_up, w_up, w_down)


def _combine_kernel(pos_ref, wts_ref, x_ref, gf_ref, ys_ref, o_ref, y1_ref, y2_ref, sem):
    tm, d = x_ref.shape
    groups = tm // SUBLANES

    def issue(g, carry):
        for u in range(SUBLANES):
            r = g * SUBLANES + u
            pltpu.make_async_copy(ys_ref.at[pl.ds(pos_ref[r], 1), :],
                                  y1_ref.at[g, pl.ds(u, 1), :], sem).start(priority=0)
            pltpu.make_async_copy(ys_ref.at[pl.ds(pos_ref[tm + r], 1), :],
                                  y2_ref.at[g, pl.ds(u, 1), :], sem).start(priority=1)
        return carry

    lax.fori_loop(0, groups, issue, 0)
    for _ in range(TOP_K):
        pltpu.make_async_copy(y1_ref, y1_ref, sem).wait()

    y1 = y1_ref[...].reshape(tm, d)
    y2 = y2_ref[...].reshape(tm, d)
    moe = wts_ref[:, 0:1] * y1 + wts_ref[:, 1:2] * y2
    o_ref[...] = _rmsnorm(x_ref[...] + moe, gf_ref[...])


def _combine(pos_flat, wts, x2, g_final, ys):
    T, D = x2.shape
    tm = TM_ROUTE
    return pl.pallas_call(
        _combine_kernel,
        grid=(T // tm,),
        in_specs=[pl.BlockSpec((TOP_K * tm,), lambda i: (i,), memory_space=pltpu.SMEM),
                  pl.BlockSpec((tm, LANES), lambda i: (i, 0)),
                  pl.BlockSpec((tm, D), lambda i: (i, 0)),
                  _const_spec(g_final.shape),
                  pl.BlockSpec(memory_space=pl.ANY)],
        out_specs=pl.BlockSpec((tm, D), lambda i: (i, 0)),
        out_shape=jax.ShapeDtypeStruct((T, D), F32),
        scratch_shapes=[pltpu.VMEM((tm // SUBLANES, SUBLANES, D), F32),
                        pltpu.VMEM((tm // SUBLANES, SUBLANES, D), F32), pltpu.SemaphoreType.DMA(())],
        compiler_params=pltpu.CompilerParams(
            dimension_semantics=("arbitrary",), vmem_limit_bytes=VMEM_LIMIT),
        name="combine",
    )(pos_flat, wts, x2, g_final, ys)


def _tile_plan(counts, cap, tm, nt):
    tiles_e = (counts + tm - 1) // tm
    ends = jnp.cumsum(tiles_e)
    total = ends[-1]
    n = jnp.minimum(jnp.arange(nt, dtype=jnp.int32), total - 1)
    e = jnp.minimum(jnp.sum(n[:, None] >= ends[None, :], axis=1).astype(jnp.int32), N_EXPERTS - 1)
    local = n - (ends[e] - tiles_e[e])
    blk = e * (cap // tm) + local
    valid = jnp.where(jnp.arange(nt) < total, jnp.clip(counts[e] - local * tm, 0, tm), 0)
    return e, blk.astype(jnp.int32), valid.astype(jnp.int32)


def _moe_layer(x2, g_ffn, w_router, w_up, w_down, g_final):
    T, D = x2.shape
    cap = T
    wr_pad = jnp.pad(w_router, ((0, 0), (0, LANES - N_EXPERTS)))
    hp, wts, pos, cnt = _router(x2, g_ffn, wr_pad, cap)
    pos_flat = pos[:, :TOP_K].reshape(T // TM_ROUTE, TM_ROUTE, TOP_K).transpose(0, 2, 1).reshape(-1)
    xs = _dispatch(pos_flat, hp, N_EXPERTS * cap)
    nt = TOP_K * T // TM_MOE + N_EXPERTS
    tile_e, tile_blk, tile_valid = _tile_plan(cnt[0, :N_EXPERTS], cap, TM_MOE, nt)
    ys = _experts(tile_e, tile_blk, tile_valid, xs, w_up, w_down)
    return _combine(pos_flat, wts, x2, g_final, ys)


def kernel(x, norm_mix, w_in, b_in, conv_w, conv_b, conv_ln_g, conv_ln_b, conv_out, pool_mix, pool_scale,
           pool_out, sgu_ln_g, sgu_ln_b, sgu_w, sgu_b, sgu_out, w_gate, b_gate, w_o, norm_ffn, ffn_w_up,
           ffn_w_down, moe_router, moe_w_up, moe_w_down, norm_final):
    B, S, D = x.shape
    row = lambda a: a.reshape(1, -1)
    c_conv = 2 * CONV_WIDTH

    def pieces(a):
        return a.reshape(a.shape[0], -1, MXU_COLS).transpose(1, 0, 2)

    for i in range(DEPTH):
        params = {
            "norm": row(norm_mix[i]),
            "w_conv_in": w_in[i, :, :c_conv].astype(BF16), "b_conv_in": row(b_in[i, :c_conv]),
            "w_zg": pieces(jnp.concatenate([w_in[i, :, c_conv:], w_gate[i]], axis=1).astype(BF16)),
            "b_zg": pieces(jnp.concatenate([b_in[i, c_conv:], b_gate[i]])[None, :]),
            "conv_w": jnp.broadcast_to(conv_w[i][:, None, :], (CONV_KERNEL, SUBLANES, CONV_WIDTH)),
            "conv_b": row(conv_b[i]),
            "conv_ln_g": row(conv_ln_g[i]), "conv_ln_b": row(conv_ln_b[i]),
            "conv_out": conv_out[i].astype(BF16),
            "pool_mix": pool_mix[i].astype(BF16), "pool_scale": row(pool_scale[i]),
            "pool_out": pool_out[i].astype(BF16),
            "sgu_ln_g": row(sgu_ln_g[i]), "sgu_ln_b": row(sgu_ln_b[i]), "sgu_w": sgu_w[i],
            "sgu_b": jnp.repeat(sgu_b[i].T, SGU_HEAD_DIM, axis=1),
            "sgu_out": sgu_out[i].astype(BF16),
            "w_o": w_o[i].astype(BF16),
        }
        x = _mixer_layer(x, params)
        x2 = x.reshape(B * S, D)
        if i % 2 == 0:
            x2 = _dense_ffn(x2, row(norm_ffn[i]), ffn_w_up[i // 2].astype(BF16),
                            ffn_w_down[i // 2].astype(BF16))
        else:
            assert i == DEPTH - 1
            x2 = _moe_layer(x2, row(norm_ffn[i]), moe_router[i // 2], moe_w_up[i // 2], moe_w_down[i // 2],
                            row(norm_final))
        x = x2.reshape(B, S, D)
    return x
```

```python
import functools

import jax
import jax.numpy as jnp
from jax import lax
from jax.experimental import pallas as pl
from jax.experimental.pallas import tpu as pltpu

D_MODEL = 1024
DEPTH = 2
CONV_WIDTH = 512
CONV_KERNEL = 31
POOL_WINDOWS = (2, 4, 8, 16)
POOL_GROUP = 128
POOL_WIDTH = POOL_GROUP * len(POOL_WINDOWS)
SGU_HEADS = 4
SGU_HEAD_DIM = 128
SGU_WIDTH = SGU_HEADS * SGU_HEAD_DIM
SGU_BLOCK = 128
N_BRANCH = 3
IN_WIDTH = 2 * CONV_WIDTH + POOL_WIDTH + 2 * SGU_WIDTH
D_FF = 2816
N_EXPERTS = 8
TOP_K = 2
D_FF_EXPERT = 3584
EPS = 1e-6

LANES = 128
SUBLANES = 8
MXU_COLS = 256
N_Z_PIECES = (POOL_WIDTH + 2 * SGU_WIDTH) // MXU_COLS
CONV_HALO = 32
POOL_HALO = 16
VMEM_LIMIT = 56 * 1024 * 1024

TM_MIX = 512
CONV_ROWS = 32
TM_FFN = 512
TM_ROUTE = 512
TM_MOE = 1024
TF_MOE = 512

F32 = jnp.float32
BF16 = jnp.bfloat16


def _const_spec(shape):
    zeros = (0,) * len(shape)
    return pl.BlockSpec(shape, lambda *_: zeros, pipeline_mode=pl.Buffered(1))


def _rmsnorm(x, g):
    return x * lax.rsqrt(jnp.mean(x * x, axis=-1, keepdims=True) + EPS) * g


def _layernorm(x, g, b):
    mu = jnp.mean(x, axis=-1, keepdims=True)
    xc = x - mu
    var = jnp.mean(xc * xc, axis=-1, keepdims=True)
    return xc * lax.rsqrt(var + EPS) * g + b


def _dot(a, b):
    return jnp.dot(a, b, preferred_element_type=F32)


def _dot_split(a, b):
    a_hi = a.astype(BF16)
    a_lo = (a - a_hi.astype(F32)).astype(BF16)
    b_hi = b.astype(BF16)
    b_lo = (b - b_hi.astype(F32)).astype(BF16)
    return _dot(a_hi, b_hi) + (_dot(a_hi, b_lo) + _dot(a_lo, b_hi))


def _mixer_kernel(x_ref, g_ref, wc_ref, bc_ref, wzg_ref, bzg_ref, cw_ref, cb_ref, clg_ref, clb_ref, cout_ref,
                  pmix_ref, pscale_ref, pout_ref, slg_ref, slb_ref, sw_ref, sb_ref, sout_ref, wo_ref,
                  o_ref, hb_buf, conv_ext, conv_sh, pool_ext, zg_buf, act_buf, m_buf):
    tm = x_ref.shape[1]
    j = pl.program_id(1)
    x = x_ref[0]
    hb_buf[...] = _rmsnorm(x, g_ref[...]).astype(BF16)

    def proj(lo, hi):
        return _dot(hb_buf[...], wc_ref[:, lo:hi]) + bc_ref[:, lo:hi]

    @pl.when(j == 0)
    def _():
        conv_ext[0:CONV_HALO, :] = jnp.zeros((CONV_HALO, CONV_WIDTH), F32)
        pool_ext[0:POOL_HALO, :] = jnp.zeros((POOL_HALO, POOL_WIDTH), F32)

    @pl.when(j > 0)
    def _():
        conv_ext[0:CONV_HALO, :] = conv_ext[tm:tm + CONV_HALO, :]
        pool_ext[0:POOL_HALO, :] = pool_ext[tm:tm + POOL_HALO, :]

    conv_ext[CONV_HALO:, :] = proj(0, CONV_WIDTH) * jax.nn.sigmoid(proj(CONV_WIDTH, 2 * CONV_WIDTH))

    n_groups = conv_sh.shape[1]
    for m in range(SUBLANES):
        groups = n_groups - (1 if m else 0)
        conv_sh[m, 0:groups] = conv_ext[m:m + groups * SUBLANES, :].reshape(groups, SUBLANES, CONV_WIDTH)

    base = CONV_HALO - (CONV_KERNEL - 1)
    gpc = CONV_ROWS // SUBLANES

    def conv_chunk(ci):
        acc = jnp.zeros((gpc, SUBLANES, CONV_WIDTH), F32) + cb_ref[...]
        for k in range(CONV_KERNEL):
            q, m = divmod(base + k, SUBLANES)
            acc = acc + conv_sh[m, pl.ds(ci * gpc + q, gpc)] * cw_ref[k]
        hn = _layernorm(acc.reshape(CONV_ROWS, CONV_WIDTH), clg_ref[...], clb_ref[...])
        rows = pl.ds(pl.multiple_of(ci * CONV_ROWS, CONV_ROWS), CONV_ROWS)
        act_buf[rows, :] = (hn * jax.nn.sigmoid(hn)).astype(BF16)

    def piece(c, act):
        zg_buf[c] = act(_dot(hb_buf[...], wzg_ref[c]) + bzg_ref[c])

    n_chunks = tm // CONV_ROWS
    n_pieces = zg_buf.shape[0]
    assert n_chunks <= n_pieces
    for c in range(n_pieces):
        piece(c, (lambda z: z) if c < N_Z_PIECES else jax.nn.sigmoid)
        if c >= n_pieces - n_chunks:
            conv_chunk(c - (n_pieces - n_chunks))

    def z_cols(lo, hi):
        return jnp.concatenate([zg_buf[c] for c in range(lo // MXU_COLS, hi // MXU_COLS)], axis=-1)

    def gate(k):
        c0 = N_Z_PIECES + k * (D_MODEL // MXU_COLS)
        return jnp.concatenate([zg_buf[c] for c in range(c0, c0 + D_MODEL // MXU_COLS)], axis=-1)

    p = z_cols(0, POOL_WIDTH)
    pool_ext[POOL_HALO:, :] = p
    head_pos = lax.broadcasted_iota(jnp.int32, (POOL_HALO, POOL_GROUP), 0)
    diffs = []
    for gi, w in enumerate(POOL_WINDOWS):
        lo, hi = gi * POOL_GROUP, (gi + 1) * POOL_GROUP
        s = pool_ext[:, lo:hi]
        step = 1
        while step < w:
            s = s + pltpu.roll(s, step, axis=0)
            step *= 2
        s = s[POOL_HALO:, :]
        mean = s * (1.0 / w)
        head_cnt = jnp.minimum(head_pos + 1, w).astype(F32)
        head = jnp.where(j == 0, s[:POOL_HALO, :] / head_cnt, mean[:POOL_HALO, :])
        mean = jnp.concatenate([head, mean[POOL_HALO:, :]], axis=0)
        diffs.append((mean - p[:, lo:hi]).astype(BF16))
    u = jax.nn.gelu(z_cols(POOL_WIDTH, POOL_WIDTH + SGU_WIDTH), approximate=True)
    mixed = [_dot(diffs[gi], pmix_ref[gi]) for gi in range(len(POOL_WINDOWS))]
    yb = jnp.concatenate(mixed, axis=-1) * pscale_ref[...]
    v = jax.nn.gelu(z_cols(POOL_WIDTH + SGU_WIDTH, POOL_WIDTH + 2 * SGU_WIDTH), approximate=True)
    vn = _layernorm(v, slg_ref[...], slb_ref[...]).astype(BF16)
    m_buf[...] = gate(1) * _dot(yb.astype(BF16), pout_ref[...])

    row = lax.broadcasted_iota(jnp.int32, (SGU_BLOCK, SGU_BLOCK), 0)
    col = lax.broadcasted_iota(jnp.int32, (SGU_BLOCK, SGU_BLOCK), 1)
    ws = [jnp.where(row >= col, sw_ref[hd], 0.0).astype(BF16) for hd in range(SGU_HEADS)]
    blocks = []
    for r in range(0, tm, SGU_BLOCK):
        heads = [_dot(ws[hd], vn[r:r + SGU_BLOCK, hd * SGU_HEAD_DIM:(hd + 1) * SGU_HEAD_DIM])
                 for hd in range(SGU_HEADS)]
        blocks.append(jnp.concatenate(heads, axis=-1) + sb_ref[...])
    y = u * jnp.concatenate(blocks, axis=0)
    m_buf[...] += gate(2) * _dot(y.astype(BF16), sout_ref[...])
    merged = m_buf[...] + gate(0) * _dot(act_buf[...], cout_ref[...])

    o_ref[0] = x + _dot(merged.astype(BF16), wo_ref[...])


def _mixer_layer(x, p):
    B, S, D = x.shape
    tm = TM_MIX
    consts = [p["norm"], p["w_conv_in"], p["b_conv_in"], p["w_zg"], p["b_zg"], p["conv_w"], p["conv_b"],
              p["conv_ln_g"], p["conv_ln_b"], p["conv_out"], p["pool_mix"], p["pool_scale"], p["pool_out"],
              p["sgu_ln_g"], p["sgu_ln_b"], p["sgu_w"], p["sgu_b"], p["sgu_out"], p["w_o"]]
    x_spec = pl.BlockSpec((1, tm, D), lambda b, j: (b, j, 0))
    n_pieces = p["w_zg"].shape[0]
    return pl.pallas_call(
        _mixer_kernel,
        grid=(B, S // tm),
        in_specs=[x_spec] + [_const_spec(c.shape) for c in consts],
        out_specs=x_spec,
        out_shape=jax.ShapeDtypeStruct(x.shape, F32),
        scratch_shapes=[pltpu.VMEM((tm, D), BF16),
                        pltpu.VMEM((tm + CONV_HALO, CONV_WIDTH), F32),
                        pltpu.VMEM((SUBLANES, (tm + CONV_HALO) // SUBLANES, SUBLANES, CONV_WIDTH), F32),
                        pltpu.VMEM((tm + POOL_HALO, POOL_WIDTH), F32),
                        pltpu.VMEM((n_pieces, tm, MXU_COLS), F32),
                        pltpu.VMEM((tm, CONV_WIDTH), BF16),
                        pltpu.VMEM((tm, D), F32)],
        compiler_params=pltpu.CompilerParams(
            dimension_semantics=("arbitrary", "arbitrary"), vmem_limit_bytes=VMEM_LIMIT),
        name="mixer",
    )(x, *consts)


def _ffn_kernel(x_ref, g_ref, wup_ref, wdn_ref, o_ref):
    x = x_ref[...]
    hb = _rmsnorm(x, g_ref[...]).astype(BF16)
    a = _dot(hb, wup_ref[:, :D_FF])
    b = _dot(hb, wup_ref[:, D_FF:])
    act = (a * jax.nn.sigmoid(a) * b).astype(BF16)
    o_ref[...] = x + _dot(act, wdn_ref[...])


def _dense_ffn(x2, g, w_up, w_down):
    T, D = x2.shape
    tm = TM_FFN
    x_spec = pl.BlockSpec((tm, D), lambda i: (i, 0))
    return pl.pallas_call(
        _ffn_kernel,
        grid=(T // tm,),
        in_specs=[x_spec, _const_spec(g.shape), _const_spec(w_up.shape), _const_spec(w_down.shape)],
        out_specs=x_spec,
        out_shape=jax.ShapeDtypeStruct(x2.shape, F32),
        compiler_params=pltpu.CompilerParams(
            dimension_semantics=("arbitrary",), vmem_limit_bytes=VMEM_LIMIT),
        name="dense_ffn",
    )(x2, g, w_up, w_down)


def _router_kernel(x_ref, g_ref, wr_ref, hp_ref, wts_ref, pos_ref, cnt_ref, carry_ref, *, cap):
    tm = x_ref.shape[0]

    @pl.when(pl.program_id(0) == 0)
    def _():
        carry_ref[...] = jnp.zeros_like(carry_ref)

    h = _rmsnorm(x_ref[...], g_ref[...])
    hp_ref[...] = h
    logits = _dot_split(h, wr_ref[...])
    lane = lax.broadcasted_iota(jnp.int32, logits.shape, 1)
    neg = jnp.float32(-jnp.inf)
    logits = jnp.where(lane < N_EXPERTS, logits, neg)
    m1 = jnp.max(logits, axis=-1, keepdims=True)
    i1 = jnp.min(jnp.where(logits == m1, lane, LANES), axis=-1, keepdims=True)
    rest = jnp.where(lane == i1, neg, logits)
    m2 = jnp.max(rest, axis=-1, keepdims=True)
    i2 = jnp.min(jnp.where(rest == m2, lane, LANES), axis=-1, keepdims=True)
    e2 = jnp.exp(m2 - m1)
    w1 = 1.0 / (1.0 + e2)
    w2 = e2 / (1.0 + e2)

    chosen = jnp.where(jnp.logical_or(lane == i1, lane == i2), 1.0, 0.0)
    r_io = lax.broadcasted_iota(jnp.int32, (tm, tm), 0)
    c_io = lax.broadcasted_iota(jnp.int32, (tm, tm), 1)
    tri = jnp.where(r_io >= c_io, 1.0, 0.0).astype(BF16)
    incl = _dot(tri, chosen.astype(BF16))
    before = incl - chosen + carry_ref[...]
    rank1 = jnp.sum(jnp.where(lane == i1, before, 0.0), axis=-1, keepdims=True).astype(jnp.int32)
    rank2 = jnp.sum(jnp.where(lane == i2, before, 0.0), axis=-1, keepdims=True).astype(jnp.int32)
    carry_ref[...] = carry_ref[...] + incl[tm - 1:tm, :]
    cnt_ref[...] = carry_ref[...].astype(jnp.int32)

    wts_ref[...] = jnp.where(lane == 0, w1, jnp.where(lane == 1, w2, 0.0))
    pos_ref[...] = jnp.where(lane == 0, i1 * cap + rank1, jnp.where(lane == 1, i2 * cap + rank2, 0))


def _router(x2, g, wr_pad, cap):
    T, D = x2.shape
    tm = TM_ROUTE
    tile = lambda w: pl.BlockSpec((tm, w), lambda i: (i, 0))
    return pl.pallas_call(
        functools.partial(_router_kernel, cap=cap),
        grid=(T // tm,),
        in_specs=[tile(D), _const_spec(g.shape), _const_spec(wr_pad.shape)],
        out_specs=[tile(D), tile(LANES), tile(LANES), pl.BlockSpec((1, LANES), lambda i: (0, 0))],
        out_shape=[jax.ShapeDtypeStruct((T, D), F32), jax.ShapeDtypeStruct((T, LANES), F32),
                   jax.ShapeDtypeStruct((T, LANES), jnp.int32), jax.ShapeDtypeStruct((1, LANES), jnp.int32)],
        scratch_shapes=[pltpu.VMEM((1, LANES), F32)],
        compiler_params=pltpu.CompilerParams(
            dimension_semantics=("arbitrary",), vmem_limit_bytes=VMEM_LIMIT),
        name="router",
    )(x2, g, wr_pad)


def _dispatch_kernel(pos_ref, hp_ref, xs_ref, sem):
    groups = hp_ref.shape[0]
    tm = groups * SUBLANES

    def issue(g, carry):
        for u in range(SUBLANES):
            r = g * SUBLANES + u
            src = hp_ref.at[g, pl.ds(u, 1), :]
            pltpu.make_async_copy(src, xs_ref.at[pl.ds(pos_ref[r], 1), :], sem).start(priority=0)
            pltpu.make_async_copy(src, xs_ref.at[pl.ds(pos_ref[tm + r], 1), :], sem).start(priority=1)
        return carry

    lax.fori_loop(0, groups, issue, 0)
    for _ in range(TOP_K):
        pltpu.make_async_copy(hp_ref, hp_ref, sem).wait()


def _dispatch(pos_flat, hp, n_rows):
    T, dp = hp.shape
    tm = TM_ROUTE
    return pl.pallas_call(
        _dispatch_kernel,
        grid=(T // tm,),
        in_specs=[pl.BlockSpec((TOP_K * tm,), lambda i: (i,), memory_space=pltpu.SMEM),
                  pl.BlockSpec((tm // SUBLANES, SUBLANES, dp), lambda i: (i, 0, 0))],
        out_specs=pl.BlockSpec(memory_space=pl.ANY),
        out_shape=jax.ShapeDtypeStruct((n_rows, dp), F32),
        scratch_shapes=[pltpu.SemaphoreType.DMA(())],
        compiler_params=pltpu.CompilerParams(
            dimension_semantics=("arbitrary",), vmem_limit_bytes=VMEM_LIMIT, has_side_effects=True),
        name="dispatch",
    )(pos_flat, hp.reshape(T // SUBLANES, SUBLANES, dp))


def _experts_kernel(te_ref, tb_ref, tv_ref, xs_ref, wa_ref, wb_ref, wd_ref, ys_ref, xbuf, xb_ref, sem):
    n, f = pl.program_id(0), pl.program_id(1)
    n_tiles = tv_ref.shape[0]
    tm = xb_ref.shape[0]
    slot = lax.rem(n, 2)
    valid = tv_ref[n]

    def fetch(tile, buf_slot):
        rows = pl.ds(pl.multiple_of(tb_ref[tile] * tm, tm), tm)
        return pltpu.make_async_copy(xs_ref.at[rows, :], xbuf.at[buf_slot], sem.at[buf_slot])

    @pl.when(f == 0)
    def _():
        @pl.when(n == 0)
        def _():
            fetch(0, 0).start()

        nxt = jnp.minimum(n + 1, n_tiles - 1)

        @pl.when(jnp.logical_and(n + 1 < n_tiles, tv_ref[nxt] > 0))
        def _():
            fetch(nxt, 1 - slot).start()

    @pl.when(valid > 0)
    def _():
        @pl.when(f == 0)
        def _():
            fetch(n, slot).wait()
            keep = lax.broadcasted_iota(jnp.int32, xb_ref.shape, 0) < valid
            xb_ref[...] = jnp.where(keep, xbuf[slot], 0.0).astype(BF16)

        xb = xb_ref[...]
        a = _dot(xb, wa_ref[0].astype(BF16))
        b = _dot(xb, wb_ref[0].astype(BF16))
        act = (a * jax.nn.sigmoid(a) * b).astype(BF16)
        part = _dot(act, wd_ref[0].astype(BF16))

        @pl.when(f == 0)
        def _():
            ys_ref[...] = part

        @pl.when(f > 0)
        def _():
            ys_ref[...] += part


def _experts(tile_e, tile_blk, tile_valid, xs, w_up, w_down):
    n_rows, dp = xs.shape
    tm, tf = TM_MOE, TF_MOE
    nf = D_FF_EXPERT // tf
    nt = tile_e.shape[0]

    def fcol(n, f, tv):
        return jnp.where(tv[n] > 0, f, nf - 1)

    grid_spec = pltpu.PrefetchScalarGridSpec(
        num_scalar_prefetch=3,
        grid=(nt, nf),
        in_specs=[pl.BlockSpec(memory_space=pl.ANY),
                  pl.BlockSpec((1, D_MODEL, tf), lambda n, f, te, tb, tv: (te[n], 0, fcol(n, f, tv))),
                  pl.BlockSpec((1, D_MODEL, tf), lambda n, f, te, tb, tv: (te[n], 0, fcol(n, f, tv) + nf)),
                  pl.BlockSpec((1, tf, D_MODEL), lambda n, f, te, tb, tv: (te[n], fcol(n, f, tv), 0))],
        out_specs=pl.BlockSpec((tm, dp), lambda n, f, te, tb, tv: (tb[n], 0)),
        scratch_shapes=[pltpu.VMEM((2, tm, D_MODEL), F32), pltpu.VMEM((tm, D_MODEL), BF16),
                        pltpu.SemaphoreType.DMA((2,))],
    )
    return pl.pallas_call(
        _experts_kernel,
        grid_spec=grid_spec,
        out_shape=jax.ShapeDtypeStruct((n_rows, dp), F32),
        compiler_params=pltpu.CompilerParams(
            dimension_semantics=("arbitrary", "arbitrary"), vmem_limit_bytes=VMEM_LIMIT),
        name="experts",
    )(tile_e, tile_blk, tile_valid, xs, w_up, w_up, w_down)


def _combine_kernel(pos_ref, wts_ref, x_ref, gf_ref, ys_ref, o_ref, y1_ref, y2_ref, sem):
    tm, d = x_ref.shape
    groups = tm // SUBLANES

    def issue(g, carry):
        for u in range(SUBLANES):
            r = g * SUBLANES + u
            pltpu.make_async_copy(ys_ref.at[pl.ds(pos_ref[r], 1), :],
                                  y1_ref.at[g, pl.ds(u, 1), :], sem).start(priority=0)
            pltpu.make_async_copy(ys_ref.at[pl.ds(pos_ref[tm + r], 1), :],
                                  y2_ref.at[g, pl.ds(u, 1), :], sem).start(priority=1)
        return carry

    lax.fori_loop(0, groups, issue, 0)
    for _ in range(TOP_K):
        pltpu.make_async_copy(y1_ref, y1_ref, sem).wait()

    y1 = y1_ref[...].reshape(tm, d)
    y2 = y2_ref[...].reshape(tm, d)
    moe = wts_ref[:, 0:1] * y1 + wts_ref[:, 1:2] * y2
    o_ref[...] = _rmsnorm(x_ref[...] + moe, gf_ref[...])


def _combine(pos_flat, wts, x2, g_final, ys):
    T, D = x2.shape
    tm = TM_ROUTE
    return pl.pallas_call(
        _combine_kernel,
        grid=(T // tm,),
        in_specs=[pl.BlockSpec((TOP_K * tm,), lambda i: (i,), memory_space=pltpu.SMEM),
                  pl.BlockSpec((tm, LANES), lambda i: (i, 0)),
                  pl.BlockSpec((tm, D), lambda i: (i, 0)),
                  _const_spec(g_final.shape),
                  pl.BlockSpec(memory_space=pl.ANY)],
        out_specs=pl.BlockSpec((tm, D), lambda i: (i, 0)),
        out_shape=jax.ShapeDtypeStruct((T, D), F32),
        scratch_shapes=[pltpu.VMEM((tm // SUBLANES, SUBLANES, D), F32),
                        pltpu.VMEM((tm // SUBLANES, SUBLANES, D), F32), pltpu.SemaphoreType.DMA(())],
        compiler_params=pltpu.CompilerParams(
            dimension_semantics=("arbitrary",), vmem_limit_bytes=VMEM_LIMIT),
        name="combine",
    )(pos_flat, wts, x2, g_final, ys)


def _tile_plan(counts, cap, tm, nt):
    tiles_e = (counts + tm - 1) // tm
    ends = jnp.cumsum(tiles_e)
    total = ends[-1]
    n = jnp.minimum(jnp.arange(nt, dtype=jnp.int32), total - 1)
    e = jnp.minimum(jnp.sum(n[:, None] >= ends[None, :], axis=1).astype(jnp.int32), N_EXPERTS - 1)
    local = n - (ends[e] - tiles_e[e])
    blk = e * (cap // tm) + local
    valid = jnp.where(jnp.arange(nt) < total, jnp.clip(counts[e] - local * tm, 0, tm), 0)
    return e, blk.astype(jnp.int32), valid.astype(jnp.int32)


def _moe_layer(x2, g_ffn, w_router, w_up, w_down, g_final):
    T, D = x2.shape
    cap = T
    wr_pad = jnp.pad(w_router, ((0, 0), (0, LANES - N_EXPERTS)))
    hp, wts, pos, cnt = _router(x2, g_ffn, wr_pad, cap)
    pos_flat = pos[:, :TOP_K].reshape(T // TM_ROUTE, TM_ROUTE, TOP_K).transpose(0, 2, 1).reshape(-1)
    xs = _dispatch(pos_flat, hp, N_EXPERTS * cap)
    nt = TOP_K * T // TM_MOE + N_EXPERTS
    tile_e, tile_blk, tile_valid = _tile_plan(cnt[0, :N_EXPERTS], cap, TM_MOE, nt)
    ys = _experts(tile_e, tile_blk, tile_valid, xs, w_up, w_down)
    return _combine(pos_flat, wts, x2, g_final, ys)


def kernel(x, norm_mix, w_in, b_in, conv_w, conv_b, conv_ln_g, conv_ln_b, conv_out, pool_mix, pool_scale,
           pool_out, sgu_ln_g, sgu_ln_b, sgu_w, sgu_b, sgu_out, w_gate, b_gate, w_o, norm_ffn, ffn_w_up,
           ffn_w_down, moe_router, moe_w_up, moe_w_down, norm_final):
    B, S, D = x.shape
    row = lambda a: a.reshape(1, -1)
    c_conv = 2 * CONV_WIDTH

    def pieces(a):
        return a.reshape(a.shape[0], -1, MXU_COLS).transpose(1, 0, 2)

    for i in range(DEPTH):
        params = {
            "norm": row(norm_mix[i]),
            "w_conv_in": w_in[i, :, :c_conv].astype(BF16), "b_conv_in": row(b_in[i, :c_conv]),
            "w_zg": pieces(jnp.concatenate([w_in[i, :, c_conv:], w_gate[i]], axis=1).astype(BF16)),
            "b_zg": pieces(jnp.concatenate([b_in[i, c_conv:], b_gate[i]])[None, :]),
            "conv_w": jnp.broadcast_to(conv_w[i][:, None, :], (CONV_KERNEL, SUBLANES, CONV_WIDTH)),
            "conv_b": row(conv_b[i]),
            "conv_ln_g": row(conv_ln_g[i]), "conv_ln_b": row(conv_ln_b[i]),
            "conv_out": conv_out[i].astype(BF16),
            "pool_mix": pool_mix[i].astype(BF16), "pool_scale": row(pool_scale[i]),
            "pool_out": pool_out[i].astype(BF16),
            "sgu_ln_g": row(sgu_ln_g[i]), "sgu_ln_b": row(sgu_ln_b[i]), "sgu_w": sgu_w[i],
            "sgu_b": jnp.repeat(sgu_b[i].T, SGU_HEAD_DIM, axis=1),
            "sgu_out": sgu_out[i].astype(BF16),
            "w_o": w_o[i].astype(BF16),
        }
        x = _mixer_layer(x, params)
        x2 = x.reshape(B * S, D)
        if i % 2 == 0:
            x2 = _dense_ffn(x2, row(norm_ffn[i]), ffn_w_up[i // 2].astype(BF16),
                            ffn_w_down[i // 2].astype(BF16))
        else:
            assert i == DEPTH - 1
            x2 = _moe_layer(x2, row(norm_ffn[i]), moe_router[i // 2], moe_w_up[i // 2], moe_w_down[i // 2],
                            row(norm_final))
        x = x2.reshape(B, S, D)
    return x
```

```python
import functools

import jax
import jax.numpy as jnp
from jax import lax
from jax.experimental import pallas as pl
from jax.experimental.pallas import tpu as pltpu

D_MODEL = 1024
DEPTH = 2
CONV_WIDTH = 512
CONV_KERNEL = 31
POOL_WINDOWS = (2, 4, 8, 16)
POOL_GROUP = 128
POOL_WIDTH = POOL_GROUP * len(POOL_WINDOWS)
SGU_HEADS = 4
SGU_HEAD_DIM = 128
SGU_WIDTH = SGU_HEADS * SGU_HEAD_DIM
SGU_BLOCK = 128
N_BRANCH = 3
IN_WIDTH = 2 * CONV_WIDTH + POOL_WIDTH + 2 * SGU_WIDTH
D_FF = 2816
N_EXPERTS = 8
TOP_K = 2
D_FF_EXPERT = 3584
EPS = 1e-6

LANES = 128
SUBLANES = 8
MXU_COLS = 256
N_Z_PIECES = (POOL_WIDTH + 2 * SGU_WIDTH) // MXU_COLS
CONV_HALO = 32
POOL_HALO = 16
VMEM_LIMIT = 56 * 1024 * 1024

TM_MIX = 512
CONV_ROWS = 32
TM_FFN = 512
TM_ROUTE = 512
TM_MOE = 1024
TF_MOE = 512

F32 = jnp.float32
BF16 = jnp.bfloat16


def _const_spec(shape):
    zeros = (0,) * len(shape)
    return pl.BlockSpec(shape, lambda *_: zeros, pipeline_mode=pl.Buffered(1))


def _rmsnorm(x, g):
    return x * lax.rsqrt(jnp.mean(x * x, axis=-1, keepdims=True) + EPS) * g


def _layernorm(x, g, b):
    mu = jnp.mean(x, axis=-1, keepdims=True)
    xc = x - mu
    var = jnp.mean(xc * xc, axis=-1, keepdims=True)
    return xc * lax.rsqrt(var + EPS) * g + b


def _dot(a, b):
    return jnp.dot(a, b, preferred_element_type=F32)


def _dot_split(a, b):
    a_hi = a.astype(BF16)
    a_lo = (a - a_hi.astype(F32)).astype(BF16)
    b_hi = b.astype(BF16)
    b_lo = (b - b_hi.astype(F32)).astype(BF16)
    return _dot(a_hi, b_hi) + (_dot(a_hi, b_lo) + _dot(a_lo, b_hi))


def _mixer_kernel(x_ref, g_ref, wc_ref, bc_ref, wzg_ref, bzg_ref, cw_ref, cb_ref, clg_ref, clb_ref, cout_ref,
                  pmix_ref, pscale_ref, pout_ref, slg_ref, slb_ref, sw_ref, sb_ref, sout_ref, wo_ref,
                  o_ref, hb_buf, conv_ext, conv_sh, pool_ext, zg_buf, act_buf, m_buf):
    tm = x_ref.shape[1]
    j = pl.program_id(1)
    x = x_ref[0]
    hb_buf[...] = _rmsnorm(x, g_ref[...]).astype(BF16)

    def proj(lo, hi):
        return _dot(hb_buf[...], wc_ref[:, lo:hi]) + bc_ref[:, lo:hi]

    @pl.when(j == 0)
    def _():
        conv_ext[0:CONV_HALO, :] = jnp.zeros((CONV_HALO, CONV_WIDTH), F32)
        pool_ext[0:POOL_HALO, :] = jnp.zeros((POOL_HALO, POOL_WIDTH), F32)

    @pl.when(j > 0)
    def _():
        conv_ext[0:CONV_HALO, :] = conv_ext[tm:tm + CONV_HALO, :]
        pool_ext[0:POOL_HALO, :] = pool_ext[tm:tm + POOL_HALO, :]

    conv_ext[CONV_HALO:, :] = proj(0, CONV_WIDTH) * jax.nn.sigmoid(proj(CONV_WIDTH, 2 * CONV_WIDTH))

    n_groups = conv_sh.shape[1]
    for m in range(SUBLANES):
        groups = n_groups - (1 if m else 0)
        conv_sh[m, 0:groups] = conv_ext[m:m + groups * SUBLANES, :].reshape(groups, SUBLANES, CONV_WIDTH)

    base = CONV_HALO - (CONV_KERNEL - 1)
    gpc = CONV_ROWS // SUBLANES

    def conv_chunk(ci):
        acc = jnp.zeros((gpc, SUBLANES, CONV_WIDTH), F32) + cb_ref[...]
        for k in range(CONV_KERNEL):
            q, m = divmod(base + k, SUBLANES)
            acc = acc + conv_sh[m, pl.ds(ci * gpc + q, gpc)] * cw_ref[k]
        hn = _layernorm(acc.reshape(CONV_ROWS, CONV_WIDTH), clg_ref[...], clb_ref[...])
        rows = pl.ds(pl.multiple_of(ci * CONV_ROWS, CONV_ROWS), CONV_ROWS)
        act_buf[rows, :] = (hn * jax.nn.sigmoid(hn)).astype(BF16)

    def piece(c, act):
        zg_buf[c] = act(_dot(hb_buf[...], wzg_ref[c]) + bzg_ref[c])

    n_chunks = tm // CONV_ROWS
    n_pieces = zg_buf.shape[0]
    assert n_chunks <= n_pieces
    for c in range(n_pieces):
        piece(c, (lambda z: z) if c < N_Z_PIECES else jax.nn.sigmoid)
        if c >= n_pieces - n_chunks:
            conv_chunk(c - (n_pieces - n_chunks))

    def z_cols(lo, hi):
        return jnp.concatenate([zg_buf[c] for c in range(lo // MXU_COLS, hi // MXU_COLS)], axis=-1)

    def gate(k):
        c0 = N_Z_PIECES + k * (D_MODEL // MXU_COLS)
        return jnp.concatenate([zg_buf[c] for c in range(c0, c0 + D_MODEL // MXU_COLS)], axis=-1)

    p = z_cols(0, POOL_WIDTH)
    pool_ext[POOL_HALO:, :] = p
    head_pos = lax.broadcasted_iota(jnp.int32, (POOL_HALO, POOL_GROUP), 0)
    diffs = []
    for gi, w in enumerate(POOL_WINDOWS):
        lo, hi = gi * POOL_GROUP, (gi + 1) * POOL_GROUP
        s = pool_ext[:, lo:hi]
        step = 1
        while step < w:
            s = s + pltpu.roll(s, step, axis=0)
            step *= 2
        s = s[POOL_HALO:, :]
        mean = s * (1.0 / w)
        head_cnt = jnp.minimum(head_pos + 1, w).astype(F32)
        head = jnp.where(j == 0, s[:POOL_HALO, :] / head_cnt, mean[:POOL_HALO, :])
        mean = jnp.concatenate([head, mean[POOL_HALO:, :]], axis=0)
        diffs.append((mean - p[:, lo:hi]).astype(BF16))
    u = jax.nn.gelu(z_cols(POOL_WIDTH, POOL_WIDTH + SGU_WIDTH), approximate=True)
    mixed = [_dot(diffs[gi], pmix_ref[gi]) for gi in range(len(POOL_WINDOWS))]
    yb = jnp.concatenate(mixed, axis=-1) * pscale_ref[...]
    v = jax.nn.gelu(z_cols(POOL_WIDTH + SGU_WIDTH, POOL_WIDTH + 2 * SGU_WIDTH), approximate=True)
    vn = _layernorm(v, slg_ref[...], slb_ref[...]).astype(BF16)
    m_buf[...] = gate(1) * _dot(yb.astype(BF16), pout_ref[...])

    row = lax.broadcasted_iota(jnp.int32, (SGU_BLOCK, SGU_BLOCK), 0)
    col = lax.broadcasted_iota(jnp.int32, (SGU_BLOCK, SGU_BLOCK), 1)
    ws = [jnp.where(row >= col, sw_ref[hd], 0.0).astype(BF16) for hd in range(SGU_HEADS)]
    blocks = []
    for r in range(0, tm, SGU_BLOCK):
        heads = [_dot(ws[hd], vn[r:r + SGU_BLOCK, hd * SGU_HEAD_DIM:(hd + 1) * SGU_HEAD_DIM])
                 for hd in range(SGU_HEADS)]
        blocks.append(jnp.concatenate(heads, axis=-1) + sb_ref[...])
    y = u * jnp.concatenate(blocks, axis=0)
    m_buf[...] += gate(2) * _dot(y.astype(BF16), sout_ref[...])
    merged = m_buf[...] + gate(0) * _dot(act_buf[...], cout_ref[...])

    o_ref[0] = x + _dot(merged.astype(BF16), wo_ref[...])


def _mixer_layer(x, p):
    B, S, D = x.shape
    tm = TM_MIX
    consts = [p["norm"], p["w_conv_in"], p["b_conv_in"], p["w_zg"], p["b_zg"], p["conv_w"], p["conv_b"],
              p["conv_ln_g"], p["conv_ln_b"], p["conv_out"], p["pool_mix"], p["pool_scale"], p["pool_out"],
              p["sgu_ln_g"], p["sgu_ln_b"], p["sgu_w"], p["sgu_b"], p["sgu_out"], p["w_o"]]
    x_spec = pl.BlockSpec((1, tm, D), lambda b, j: (b, j, 0))
    n_pieces = p["w_zg"].shape[0]
    return pl.pallas_call(
        _mixer_kernel,
        grid=(B, S // tm),
        in_specs=[x_spec] + [_const_spec(c.shape) for c in consts],
        out_specs=x_spec,
        out_shape=jax.ShapeDtypeStruct(x.shape, F32),
        scratch_shapes=[pltpu.VMEM((tm, D), BF16),
                        pltpu.VMEM((tm + CONV_HALO, CONV_WIDTH), F32),
                        pltpu.VMEM((SUBLANES, (tm + CONV_HALO) // SUBLANES, SUBLANES, CONV_WIDTH), F32),
                        pltpu.VMEM((tm + POOL_HALO, POOL_WIDTH), F32),
                        pltpu.VMEM((n_pieces, tm, MXU_COLS), F32),
                        pltpu.VMEM((tm, CONV_WIDTH), BF16),
                        pltpu.VMEM((tm, D), F32)],
        compiler_params=pltpu.CompilerParams(
            dimension_semantics=("arbitrary", "arbitrary"), vmem_limit_bytes=VMEM_LIMIT),
        name="mixer",
    )(x, *consts)


def _ffn_kernel(x_ref, g_ref, wup_ref, wdn_ref, o_ref):
    x = x_ref[...]
    hb = _rmsnorm(x, g_ref[...]).astype(BF16)
    a = _dot(hb, wup_ref[:, :D_FF])
    b = _dot(hb, wup_ref[:, D_FF:])
    act = (a * jax.nn.sigmoid(a) * b).astype(BF16)
    o_ref[...] = x + _dot(act, wdn_ref[...])


def _dense_ffn(x2, g, w_up, w_down):
    T, D = x2.shape
    tm = TM_FFN
    x_spec = pl.BlockSpec((tm, D), lambda i: (i, 0))
    return pl.pallas_call(
        _ffn_kernel,
        grid=(T // tm,),
        in_specs=[x_spec, _const_spec(g.shape), _const_spec(w_up.shape), _const_spec(w_down.shape)],
        out_specs=x_spec,
        out_shape=jax.ShapeDtypeStruct(x2.shape, F32),
        compiler_params=pltpu.CompilerParams(
            dimension_semantics=("arbitrary",), vmem_limit_bytes=VMEM_LIMIT),
        name="dense_ffn",
    )(x2, g, w_up, w_down)


def _router_kernel(x_ref, g_ref, wr_ref, hp_ref, wts_ref, pos_ref, cnt_ref, carry_ref, *, cap):
    tm = x_ref.shape[0]

    @pl.when(pl.program_id(0) == 0)
    def _():
        carry_ref[...] = jnp.zeros_like(carry_ref)

    h = _rmsnorm(x_ref[...], g_ref[...])
    hp_ref[...] = h
    logits = _dot_split(h, wr_ref[...])
    lane = lax.broadcasted_iota(jnp.int32, logits.shape, 1)
    neg = jnp.float32(-jnp.inf)
    logits = jnp.where(lane < N_EXPERTS, logits, neg)
    m1 = jnp.max(logits, axis=-1, keepdims=True)
    i1 = jnp.min(jnp.where(logits == m1, lane, LANES), axis=-1, keepdims=True)
    rest = jnp.where(lane == i1, neg, logits)
    m2 = jnp.max(rest, axis=-1, keepdims=True)
    i2 = jnp.min(jnp.where(rest == m2, lane, LANES), axis=-1, keepdims=True)
    e2 = jnp.exp(m2 - m1)
    w1 = 1.0 / (1.0 + e2)
    w2 = e2 / (1.0 + e2)

    chosen = jnp.where(jnp.logical_or(lane == i1, lane == i2), 1.0, 0.0)
    r_io = lax.broadcasted_iota(jnp.int32, (tm, tm), 0)
    c_io = lax.broadcasted_iota(jnp.int32, (tm, tm), 1)
    tri = jnp.where(r_io >= c_io, 1.0, 0.0).astype(BF16)
    incl = _dot(tri, chosen.astype(BF16))
    before = incl - chosen + carry_ref[...]
    rank1 = jnp.sum(jnp.where(lane == i1, before, 0.0), axis=-1, keepdims=True).astype(jnp.int32)
    rank2 = jnp.sum(jnp.where(lane == i2, before, 0.0), axis=-1, keepdims=True).astype(jnp.int32)
    carry_ref[...] = carry_ref[...] + incl[tm - 1:tm, :]
    cnt_ref[...] = carry_ref[...].astype(jnp.int32)

    wts_ref[...] = jnp.where(lane == 0, w1, jnp.where(lane == 1, w2, 0.0))
    pos_ref[...] = jnp.where(lane == 0, i1 * cap + rank1, jnp.where(lane == 1, i2 * cap + rank2, 0))


def _router(x2, g, wr_pad, cap):
    T, D = x2.shape
    tm = TM_ROUTE
    tile = lambda w: pl.BlockSpec((tm, w), lambda i: (i, 0))
    return pl.pallas_call(
        functools.partial(_router_kernel, cap=cap),
        grid=(T // tm,),
        in_specs=[tile(D), _const_spec(g.shape), _const_spec(wr_pad.shape)],
        out_specs=[tile(D), tile(LANES), tile(LANES), pl.BlockSpec((1, LANES), lambda i: (0, 0))],
        out_shape=[jax.ShapeDtypeStruct((T, D), F32), jax.ShapeDtypeStruct((T, LANES), F32),
                   jax.ShapeDtypeStruct((T, LANES), jnp.int32), jax.ShapeDtypeStruct((1, LANES), jnp.int32)],
        scratch_shapes=[pltpu.VMEM((1, LANES), F32)],
        compiler_params=pltpu.CompilerParams(
            dimension_semantics=("arbitrary",), vmem_limit_bytes=VMEM_LIMIT),
        name="router",
    )(x2, g, wr_pad)


def _dispatch_kernel(pos_ref, hp_ref, xs_ref, sem):
    groups = hp_ref.shape[0]
    tm = groups * SUBLANES

    def issue(g, carry):
        for u in range(SUBLANES):
            r = g * SUBLANES + u
            src = hp_ref.at[g, pl.ds(u, 1), :]
            pltpu.make_async_copy(src, xs_ref.at[pl.ds(pos_ref[r], 1), :], sem).start(priority=0)
            pltpu.make_async_copy(src, xs_ref.at[pl.ds(pos_ref[tm + r], 1), :], sem).start(priority=1)
        return carry

    lax.fori_loop(0, groups, issue, 0)
    for _ in range(TOP_K):
        pltpu.make_async_copy(hp_ref, hp_ref, sem).wait()


def _dispatch(pos_flat, hp, n_rows):
    T, dp = hp.shape
    tm = TM_ROUTE
    return pl.pallas_call(
        _dispatch_kernel,
        grid=(T // tm,),
        in_specs=[pl.BlockSpec((TOP_K * tm,), lambda i: (i,), memory_space=pltpu.SMEM),
                  pl.BlockSpec((tm // SUBLANES, SUBLANES, dp), lambda i: (i, 0, 0))],
        out_specs=pl.BlockSpec(memory_space=pl.ANY),
        out_shape=jax.ShapeDtypeStruct((n_rows, dp), F32),
        scratch_shapes=[pltpu.SemaphoreType.DMA(())],
        compiler_params=pltpu.CompilerParams(
            dimension_semantics=("arbitrary",), vmem_limit_bytes=VMEM_LIMIT, has_side_effects=True),
        name="dispatch",
    )(pos_flat, hp.reshape(T // SUBLANES, SUBLANES, dp))


def _experts_kernel(te_ref, tb_ref, tv_ref, xs_ref, wa_ref, wb_ref, wd_ref, ys_ref, xbuf, xb_ref, sem):
    n, f = pl.program_id(0), pl.program_id(1)
    n_tiles = tv_ref.shape[0]
    tm = xb_ref.shape[0]
    slot = lax.rem(n, 2)
    valid = tv_ref[n]

    def fetch(tile, buf_slot):
        rows = pl.ds(pl.multiple_of(tb_ref[tile] * tm, tm), tm)
        return pltpu.make_async_copy(xs_ref.at[rows, :], xbuf.at[buf_slot], sem.at[buf_slot])

    @pl.when(jnp.logical_and(n == 0, f == 0))
    def _():
        fetch(0, 0).start()

    nxt = jnp.minimum(n + 1, n_tiles - 1)

    @pl.when(jnp.logical_and(f == pl.num_programs(1) // 2,
                             jnp.logical_and(n + 1 < n_tiles, tv_ref[nxt] > 0)))
    def _():
        fetch(nxt, 1 - slot).start()

    @pl.when(valid > 0)
    def _():
        @pl.when(f == 0)
        def _():
            fetch(n, slot).wait()
            keep = lax.broadcasted_iota(jnp.int32, xb_ref.shape, 0) < valid
            xb_ref[...] = jnp.where(keep, xbuf[slot], 0.0).astype(BF16)
            ys_ref[...] = jnp.zeros_like(ys_ref)

        xb = xb_ref[...]
        a = _dot(xb, wa_ref[0].astype(BF16))
        b = _dot(xb, wb_ref[0].astype(BF16))
        act = (a * jax.nn.sigmoid(a) * b).astype(BF16)
        ys_ref[...] += _dot(act, wd_ref[0].astype(BF16))


def _experts(tile_e, tile_blk, tile_valid, xs, w_up, w_down):
    n_rows, dp = xs.shape
    tm, tf = TM_MOE, TF_MOE
    nf = D_FF_EXPERT // tf
    nt = tile_e.shape[0]

    def fcol(n, f, tv):
        return jnp.where(tv[n] > 0, f, nf - 1)

    grid_spec = pltpu.PrefetchScalarGridSpec(
        num_scalar_prefetch=3,
        grid=(nt, nf),
        in_specs=[pl.BlockSpec(memory_space=pl.ANY),
                  pl.BlockSpec((1, D_MODEL, tf), lambda n, f, te, tb, tv: (te[n], 0, fcol(n, f, tv))),
                  pl.BlockSpec((1, D_MODEL, tf), lambda n, f, te, tb, tv: (te[n], 0, fcol(n, f, tv) + nf)),
                  pl.BlockSpec((1, tf, D_MODEL), lambda n, f, te, tb, tv: (te[n], fcol(n, f, tv), 0))],
        out_specs=pl.BlockSpec((tm, dp), lambda n, f, te, tb, tv: (tb[n], 0)),
        scratch_shapes=[pltpu.VMEM((2, tm, D_MODEL), F32), pltpu.VMEM((tm, D_MODEL), BF16),
                        pltpu.SemaphoreType.DMA((2,))],
    )
    return pl.pallas_call(
        _experts_kernel,
        grid_spec=grid_spec,
        out_shape=jax.ShapeDtypeStruct((n_rows, dp), F32),
        compiler_params=pltpu.CompilerParams(
            dimension_semantics=("arbitrary", "arbitrary"), vmem_limit_bytes=VMEM_LIMIT),
        name="experts",
    )(tile_e, tile_blk, tile_valid, xs, w_up, w_up, w_down)


def _combine_kernel(pos_ref, wts_ref, x_ref, gf_ref, ys_ref, o_ref, y1_ref, y2_ref, sem):
    tm, d = x_ref.shape
    groups = tm // SUBLANES

    def issue(g, carry):
        for u in range(SUBLANES):
            r = g * SUBLANES + u
            pltpu.make_async_copy(ys_ref.at[pl.ds(pos_ref[r], 1), :],
                                  y1_ref.at[g, pl.ds(u, 1), :], sem).start(priority=0)
            pltpu.make_async_copy(ys_ref.at[pl.ds(pos_ref[tm + r], 1), :],
                                  y2_ref.at[g, pl.ds(u, 1), :], sem).start(priority=1)
        return carry

    lax.fori_loop(0, groups, issue, 0)
    for _ in range(TOP_K):
        pltpu.make_async_copy(y1_ref, y1_ref, sem).wait()

    y1 = y1_ref[...].reshape(tm, d)
    y2 = y2_ref[...].reshape(tm, d)
    moe = wts_ref[:, 0:1] * y1 + wts_ref[:, 1:2] * y2
    o_ref[...] = _rmsnorm(x_ref[...] + moe, gf_ref[...])


def _combine(pos_flat, wts, x2, g_final, ys):
    T, D = x2.shape
    tm = TM_ROUTE
    return pl.pallas_call(
        _combine_kernel,
        grid=(T // tm,),
        in_specs=[pl.BlockSpec((TOP_K * tm,), lambda i: (i,), memory_space=pltpu.SMEM),
                  pl.BlockSpec((tm, LANES), lambda i: (i, 0)),
                  pl.BlockSpec((tm, D), lambda i: (i, 0)),
                  _const_spec(g_final.shape),
                  pl.BlockSpec(memory_space=pl.ANY)],
        out_specs=pl.BlockSpec((tm, D), lambda i: (i, 0)),
        out_shape=jax.ShapeDtypeStruct((T, D), F32),
        scratch_shapes=[pltpu.VMEM((tm // SUBLANES, SUBLANES, D), F32),
                        pltpu.VMEM((tm // SUBLANES, SUBLANES, D), F32), pltpu.SemaphoreType.DMA(())],
        compiler_params=pltpu.CompilerParams(
            dimension_semantics=("arbitrary",), vmem_limit_bytes=VMEM_LIMIT),
        name="combine",
    )(pos_flat, wts, x2, g_final, ys)


def _tile_plan(counts, cap, tm, nt):
    tiles_e = (counts + tm - 1) // tm
    ends = jnp.cumsum(tiles_e)
    total = ends[-1]
    n = jnp.minimum(jnp.arange(nt, dtype=jnp.int32), total - 1)
    e = jnp.minimum(jnp.sum(n[:, None] >= ends[None, :], axis=1).astype(jnp.int32), N_EXPERTS - 1)
    local = n - (ends[e] - tiles_e[e])
    blk = e * (cap // tm) + local
    valid = jnp.where(jnp.arange(nt) < total, jnp.clip(counts[e] - local * tm, 0, tm), 0)
    return e, blk.astype(jnp.int32), valid.astype(jnp.int32)


def _moe_layer(x2, g_ffn, w_router, w_up, w_down, g_final):
    T, D = x2.shape
    cap = T
    wr_pad = jnp.pad(w_router, ((0, 0), (0, LANES - N_EXPERTS)))
    hp, wts, pos, cnt = _router(x2, g_ffn, wr_pad, cap)
    pos_flat = pos[:, :TOP_K].reshape(T // TM_ROUTE, TM_ROUTE, TOP_K).transpose(0, 2, 1).reshape(-1)
    xs = _dispatch(pos_flat, hp, N_EXPERTS * cap)
    nt = TOP_K * T // TM_MOE + N_EXPERTS
    tile_e, tile_blk, tile_valid = _tile_plan(cnt[0, :N_EXPERTS], cap, TM_MOE, nt)
    ys = _experts(tile_e, tile_blk, tile_valid, xs, w_up, w_down)
    return _combine(pos_flat, wts, x2, g_final, ys)


def kernel(x, norm_mix, w_in, b_in, conv_w, conv_b, conv_ln_g, conv_ln_b, conv_out, pool_mix, pool_scale,
           pool_out, sgu_ln_g, sgu_ln_b, sgu_w, sgu_b, sgu_out, w_gate, b_gate, w_o, norm_ffn, ffn_w_up,
           ffn_w_down, moe_router, moe_w_up, moe_w_down, norm_final):
    B, S, D = x.shape
    row = lambda a: a.reshape(1, -1)
    c_conv = 2 * CONV_WIDTH

    def pieces(a):
        return a.reshape(a.shape[0], -1, MXU_COLS).transpose(1, 0, 2)

    for i in range(DEPTH):
        params = {
            "norm": row(norm_mix[i]),
            "w_conv_in": w_in[i, :, :c_conv].astype(BF16), "b_conv_in": row(b_in[i, :c_conv]),
            "w_zg": pieces(jnp.concatenate([w_in[i, :, c_conv:], w_gate[i]], axis=1).astype(BF16)),
            "b_zg": pieces(jnp.concatenate([b_in[i, c_conv:], b_gate[i]])[None, :]),
            "conv_w": jnp.broadcast_to(conv_w[i][:, None, :], (CONV_KERNEL, SUBLANES, CONV_WIDTH)),
            "conv_b": row(conv_b[i]),
            "conv_ln_g": row(conv_ln_g[i]), "conv_ln_b": row(conv_ln_b[i]),
            "conv_out": conv_out[i].astype(BF16),
            "pool_mix": pool_mix[i].astype(BF16), "pool_scale": row(pool_scale[i]),
            "pool_out": pool_out[i].astype(BF16),
            "sgu_ln_g": row(sgu_ln_g[i]), "sgu_ln_b": row(sgu_ln_b[i]), "sgu_w": sgu_w[i],
            "sgu_b": jnp.repeat(sgu_b[i].T, SGU_HEAD_DIM, axis=1),
            "sgu_out": sgu_out[i].astype(BF16),
            "w_o": w_o[i].astype(BF16),
        }
        x = _mixer_layer(x, params)
        x2 = x.reshape(B * S, D)
        if i % 2 == 0:
            x2 = _dense_ffn(x2, row(norm_ffn[i]), ffn_w_up[i // 2].astype(BF16),
                            ffn_w_down[i // 2].astype(BF16))
        else:
            assert i == DEPTH - 1
            x2 = _moe_layer(x2, row(norm_ffn[i]), moe_router[i // 2], moe_w_up[i // 2], moe_w_down[i // 2],
                            row(norm_final))
        x = x2.reshape(B, S, D)
    return x
```

```python
import functools

import jax
import jax.numpy as jnp
from jax import lax
from jax.experimental import pallas as pl
from jax.experimental.pallas import tpu as pltpu

D_MODEL = 1024
DEPTH = 2
CONV_WIDTH = 512
CONV_KERNEL = 31
POOL_WINDOWS = (2, 4, 8, 16)
POOL_GROUP = 128
POOL_WIDTH = POOL_GROUP * len(POOL_WINDOWS)
SGU_HEADS = 4
SGU_HEAD_DIM = 128
SGU_WIDTH = SGU_HEADS * SGU_HEAD_DIM
SGU_BLOCK = 128
N_BRANCH = 3
IN_WIDTH = 2 * CONV_WIDTH + POOL_WIDTH + 2 * SGU_WIDTH
D_FF = 2816
N_EXPERTS = 8
TOP_K = 2
D_FF_EXPERT = 3584
EPS = 1e-6

LANES = 128
SUBLANES = 8
MXU_COLS = 256
N_Z_PIECES = (POOL_WIDTH + 2 * SGU_WIDTH) // MXU_COLS
CONV_HALO = 32
POOL_HALO = 16
VMEM_LIMIT = 56 * 1024 * 1024

TM_MIX = 512
CONV_ROWS = 32
TM_FFN = 512
TM_ROUTE = 512
TM_MOE = 1024
TF_MOE = 512

F32 = jnp.float32
BF16 = jnp.bfloat16


def _const_spec(shape):
    zeros = (0,) * len(shape)
    return pl.BlockSpec(shape, lambda *_: zeros, pipeline_mode=pl.Buffered(1))


def _rmsnorm(x, g):
    return x * lax.rsqrt(jnp.mean(x * x, axis=-1, keepdims=True) + EPS) * g


def _layernorm(x, g, b):
    mu = jnp.mean(x, axis=-1, keepdims=True)
    xc = x - mu
    var = jnp.mean(xc * xc, axis=-1, keepdims=True)
    return xc * lax.rsqrt(var + EPS) * g + b


def _dot(a, b):
    return jnp.dot(a, b, preferred_element_type=F32)


def _dot_split(a, b):
    a_hi = a.astype(BF16)
    a_lo = (a - a_hi.astype(F32)).astype(BF16)
    b_hi = b.astype(BF16)
    b_lo = (b - b_hi.astype(F32)).astype(BF16)
    return _dot(a_hi, b_hi) + (_dot(a_hi, b_lo) + _dot(a_lo, b_hi))


def _mixer_kernel(x_ref, g_ref, wc_ref, bc_ref, wzg_ref, bzg_ref, cw_ref, cb_ref, clg_ref, clb_ref, cout_ref,
                  pmix_ref, pscale_ref, pout_ref, slg_ref, slb_ref, sw_ref, sb_ref, sout_ref, wo_ref,
                  o_ref, hb_buf, conv_ext, conv_sh, pool_ext, zg_buf, act_buf, m_buf):
    tm = x_ref.shape[1]
    j = pl.program_id(1)

    @pl.when(j == 0)
    def _():
        conv_ext[0:CONV_HALO, :] = jnp.zeros((CONV_HALO, CONV_WIDTH), F32)
        pool_ext[0:POOL_HALO, :] = jnp.zeros((POOL_HALO, POOL_WIDTH), F32)

    @pl.when(j > 0)
    def _():
        conv_ext[0:CONV_HALO, :] = conv_ext[tm:tm + CONV_HALO, :]
        pool_ext[0:POOL_HALO, :] = pool_ext[tm:tm + POOL_HALO, :]

    x = x_ref[0]
    hb_buf[...] = _rmsnorm(x, g_ref[...]).astype(BF16)

    def proj(lo, hi):
        return _dot(hb_buf[...], wc_ref[:, lo:hi]) + bc_ref[:, lo:hi]

    conv_ext[CONV_HALO:, :] = proj(0, CONV_WIDTH) * jax.nn.sigmoid(proj(CONV_WIDTH, 2 * CONV_WIDTH))

    n_groups = conv_sh.shape[1]
    for m in range(SUBLANES):
        groups = n_groups - (1 if m else 0)
        conv_sh[m, 0:groups] = conv_ext[m:m + groups * SUBLANES, :].reshape(groups, SUBLANES, CONV_WIDTH)

    base = CONV_HALO - (CONV_KERNEL - 1)
    gpc = CONV_ROWS // SUBLANES

    def conv_chunk(ci):
        acc = jnp.zeros((gpc, SUBLANES, CONV_WIDTH), F32) + cb_ref[...]
        for k in range(CONV_KERNEL):
            q, m = divmod(base + k, SUBLANES)
            acc = acc + conv_sh[m, pl.ds(ci * gpc + q, gpc)] * cw_ref[k]
        hn = _layernorm(acc.reshape(CONV_ROWS, CONV_WIDTH), clg_ref[...], clb_ref[...])
        rows = pl.ds(pl.multiple_of(ci * CONV_ROWS, CONV_ROWS), CONV_ROWS)
        act_buf[rows, :] = (hn * jax.nn.sigmoid(hn)).astype(BF16)

    def piece(c, act):
        cols = slice(c * MXU_COLS, (c + 1) * MXU_COLS)
        zg_buf[c] = act(_dot(hb_buf[...], wzg_ref[:, cols]) + bzg_ref[:, cols])

    n_chunks = tm // CONV_ROWS
    n_pieces = zg_buf.shape[0]
    assert n_chunks <= n_pieces
    for c in range(n_pieces):
        piece(c, (lambda z: z) if c < N_Z_PIECES else jax.nn.sigmoid)
        if c >= n_pieces - n_chunks:
            conv_chunk(c - (n_pieces - n_chunks))

    def z_cols(lo, hi):
        return jnp.concatenate([zg_buf[c] for c in range(lo // MXU_COLS, hi // MXU_COLS)], axis=-1)

    def gate(k):
        c0 = N_Z_PIECES + k * (D_MODEL // MXU_COLS)
        return jnp.concatenate([zg_buf[c] for c in range(c0, c0 + D_MODEL // MXU_COLS)], axis=-1)

    p = z_cols(0, POOL_WIDTH)
    pool_ext[POOL_HALO:, :] = p
    head_pos = lax.broadcasted_iota(jnp.int32, (POOL_HALO, POOL_GROUP), 0)
    diffs = []
    for gi, w in enumerate(POOL_WINDOWS):
        lo, hi = gi * POOL_GROUP, (gi + 1) * POOL_GROUP
        s = pool_ext[:, lo:hi]
        step = 1
        while step < w:
            s = s + pltpu.roll(s, step, axis=0)
            step *= 2
        s = s[POOL_HALO:, :]
        mean = s * (1.0 / w)
        head_cnt = jnp.minimum(head_pos + 1, w).astype(F32)
        head = jnp.where(j == 0, s[:POOL_HALO, :] / head_cnt, mean[:POOL_HALO, :])
        mean = jnp.concatenate([head, mean[POOL_HALO:, :]], axis=0)
        diffs.append((mean - p[:, lo:hi]).astype(BF16))
    u = jax.nn.gelu(z_cols(POOL_WIDTH, POOL_WIDTH + SGU_WIDTH), approximate=True)
    mixed = [_dot(diffs[gi], pmix_ref[gi]) for gi in range(len(POOL_WINDOWS))]
    yb = jnp.concatenate(mixed, axis=-1) * pscale_ref[...]
    v = jax.nn.gelu(z_cols(POOL_WIDTH + SGU_WIDTH, POOL_WIDTH + 2 * SGU_WIDTH), approximate=True)
    vn = _layernorm(v, slg_ref[...], slb_ref[...]).astype(BF16)
    m_buf[...] = gate(1) * _dot(yb.astype(BF16), pout_ref[...])

    row = lax.broadcasted_iota(jnp.int32, (SGU_BLOCK, SGU_BLOCK), 0)
    col = lax.broadcasted_iota(jnp.int32, (SGU_BLOCK, SGU_BLOCK), 1)
    ws = [jnp.where(row >= col, sw_ref[hd], 0.0).astype(BF16) for hd in range(SGU_HEADS)]
    blocks = []
    for r in range(0, tm, SGU_BLOCK):
        heads = [_dot(ws[hd], vn[r:r + SGU_BLOCK, hd * SGU_HEAD_DIM:(hd + 1) * SGU_HEAD_DIM])
                 for hd in range(SGU_HEADS)]
        blocks.append(jnp.concatenate(heads, axis=-1) + sb_ref[...])
    y = u * jnp.concatenate(blocks, axis=0)
    m_buf[...] += gate(2) * _dot(y.astype(BF16), sout_ref[...])
    merged = m_buf[...] + gate(0) * _dot(act_buf[...], cout_ref[...])

    o_ref[0] = x + _dot(merged.astype(BF16), wo_ref[...])


def _mixer_layer(x, p):
    B, S, D = x.shape
    tm = TM_MIX
    consts = [p["norm"], p["w_conv_in"], p["b_conv_in"], p["w_zg"], p["b_zg"], p["conv_w"], p["conv_b"],
              p["conv_ln_g"], p["conv_ln_b"], p["conv_out"], p["pool_mix"], p["pool_scale"], p["pool_out"],
              p["sgu_ln_g"], p["sgu_ln_b"], p["sgu_w"], p["sgu_b"], p["sgu_out"], p["w_o"]]
    x_spec = pl.BlockSpec((1, tm, D), lambda b, j: (b, j, 0))
    n_pieces = p["w_zg"].shape[1] // MXU_COLS
    return pl.pallas_call(
        _mixer_kernel,
        grid=(B, S // tm),
        in_specs=[x_spec] + [_const_spec(c.shape) for c in consts],
        out_specs=x_spec,
        out_shape=jax.ShapeDtypeStruct(x.shape, F32),
        scratch_shapes=[pltpu.VMEM((tm, D), BF16),
                        pltpu.VMEM((tm + CONV_HALO, CONV_WIDTH), F32),
                        pltpu.VMEM((SUBLANES, (tm + CONV_HALO) // SUBLANES, SUBLANES, CONV_WIDTH), F32),
                        pltpu.VMEM((tm + POOL_HALO, POOL_WIDTH), F32),
                        pltpu.VMEM((n_pieces, tm, MXU_COLS), F32),
                        pltpu.VMEM((tm, CONV_WIDTH), BF16),
                        pltpu.VMEM((tm, D), F32)],
        compiler_params=pltpu.CompilerParams(
            dimension_semantics=("arbitrary", "arbitrary"), vmem_limit_bytes=VMEM_LIMIT),
        name="mixer",
    )(x, *consts)


def _ffn_kernel(x_ref, g_ref, wup_ref, wdn_ref, o_ref):
    x = x_ref[...]
    hb = _rmsnorm(x, g_ref[...]).astype(BF16)
    a = _dot(hb, wup_ref[:, :D_FF])
    b = _dot(hb, wup_ref[:, D_FF:])
    act = (a * jax.nn.sigmoid(a) * b).astype(BF16)
    o_ref[...] = x + _dot(act, wdn_ref[...])


def _dense_ffn(x2, g, w_up, w_down):
    T, D = x2.shape
    tm = TM_FFN
    x_spec = pl.BlockSpec((tm, D), lambda i: (i, 0))
    return pl.pallas_call(
        _ffn_kernel,
        grid=(T // tm,),
        in_specs=[x_spec, _const_spec(g.shape), _const_spec(w_up.shape), _const_spec(w_down.shape)],
        out_specs=x_spec,
        out_shape=jax.ShapeDtypeStruct(x2.shape, F32),
        compiler_params=pltpu.CompilerParams(
            dimension_semantics=("arbitrary",), vmem_limit_bytes=VMEM_LIMIT),
        name="dense_ffn",
    )(x2, g, w_up, w_down)


def _router_kernel(x_ref, g_ref, wr_ref, hp_ref, wts_ref, pos_ref, cnt_ref, carry_ref, *, cap):
    tm = x_ref.shape[0]

    @pl.when(pl.program_id(0) == 0)
    def _():
        carry_ref[...] = jnp.zeros_like(carry_ref)

    h = _rmsnorm(x_ref[...], g_ref[...])
    hp_ref[...] = h
    logits = _dot_split(h, wr_ref[...])
    lane = lax.broadcasted_iota(jnp.int32, logits.shape, 1)
    neg = jnp.float32(-jnp.inf)
    logits = jnp.where(lane < N_EXPERTS, logits, neg)
    m1 = jnp.max(logits, axis=-1, keepdims=True)
    i1 = jnp.min(jnp.where(logits == m1, lane, LANES), axis=-1, keepdims=True)
    rest = jnp.where(lane == i1, neg, logits)
    m2 = jnp.max(rest, axis=-1, keepdims=True)
    i2 = jnp.min(jnp.where(rest == m2, lane, LANES), axis=-1, keepdims=True)
    e2 = jnp.exp(m2 - m1)
    w1 = 1.0 / (1.0 + e2)
    w2 = e2 / (1.0 + e2)

    chosen = jnp.where(jnp.logical_or(lane == i1, lane == i2), 1.0, 0.0)
    r_io = lax.broadcasted_iota(jnp.int32, (tm, tm), 0)
    c_io = lax.broadcasted_iota(jnp.int32, (tm, tm), 1)
    tri = jnp.where(r_io >= c_io, 1.0, 0.0).astype(BF16)
    incl = _dot(tri, chosen.astype(BF16))
    before = incl - chosen + carry_ref[...]
    rank1 = jnp.sum(jnp.where(lane == i1, before, 0.0), axis=-1, keepdims=True).astype(jnp.int32)
    rank2 = jnp.sum(jnp.where(lane == i2, before, 0.0), axis=-1, keepdims=True).astype(jnp.int32)
    carry_ref[...] = carry_ref[...] + incl[tm - 1:tm, :]
    cnt_ref[...] = carry_ref[...].astype(jnp.int32)

    wts_ref[...] = jnp.where(lane == 0, w1, jnp.where(lane == 1, w2, 0.0))
    pos_ref[...] = jnp.where(lane == 0, i1 * cap + rank1, jnp.where(lane == 1, i2 * cap + rank2, 0))


def _router(x2, g, wr_pad, cap):
    T, D = x2.shape
    tm = TM_ROUTE
    tile = lambda w: pl.BlockSpec((tm, w), lambda i: (i, 0))
    return pl.pallas_call(
        functools.partial(_router_kernel, cap=cap),
        grid=(T // tm,),
        in_specs=[tile(D), _const_spec(g.shape), _const_spec(wr_pad.shape)],
        out_specs=[tile(D), tile(LANES), tile(LANES), pl.BlockSpec((1, LANES), lambda i: (0, 0))],
        out_shape=[jax.ShapeDtypeStruct((T, D), F32), jax.ShapeDtypeStruct((T, LANES), F32),
                   jax.ShapeDtypeStruct((T, LANES), jnp.int32), jax.ShapeDtypeStruct((1, LANES), jnp.int32)],
        scratch_shapes=[pltpu.VMEM((1, LANES), F32)],
        compiler_params=pltpu.CompilerParams(
            dimension_semantics=("arbitrary",), vmem_limit_bytes=VMEM_LIMIT),
        name="router",
    )(x2, g, wr_pad)


def _dispatch_kernel(pos_ref, hp_ref, xs_ref, sem):
    groups = hp_ref.shape[0]
    tm = groups * SUBLANES

    def issue(g, carry):
        for u in range(SUBLANES):
            r = g * SUBLANES + u
            src = hp_ref.at[g, pl.ds(u, 1), :]
            pltpu.make_async_copy(src, xs_ref.at[pl.ds(pos_ref[r], 1), :], sem).start(priority=0)
            pltpu.make_async_copy(src, xs_ref.at[pl.ds(pos_ref[tm + r], 1), :], sem).start(priority=1)
        return carry

    lax.fori_loop(0, groups, issue, 0)
    for _ in range(TOP_K):
        pltpu.make_async_copy(hp_ref, hp_ref, sem).wait()


def _dispatch(pos_flat, hp, n_rows):
    T, dp = hp.shape
    tm = TM_ROUTE
    return pl.pallas_call(
        _dispatch_kernel,
        grid=(T // tm,),
        in_specs=[pl.BlockSpec((TOP_K * tm,), lambda i: (i,), memory_space=pltpu.SMEM),
                  pl.BlockSpec((tm // SUBLANES, SUBLANES, dp), lambda i: (i, 0, 0))],
        out_specs=pl.BlockSpec(memory_space=pl.ANY),
        out_shape=jax.ShapeDtypeStruct((n_rows, dp), F32),
        scratch_shapes=[pltpu.SemaphoreType.DMA(())],
        compiler_params=pltpu.CompilerParams(
            dimension_semantics=("arbitrary",), vmem_limit_bytes=VMEM_LIMIT, has_side_effects=True),
        name="dispatch",
    )(pos_flat, hp.reshape(T // SUBLANES, SUBLANES, dp))


def _experts_kernel(te_ref, tb_ref, tv_ref, xs_ref, wa_ref, wb_ref, wd_ref, ys_ref, xbuf, xb_ref, sem):
    n, f = pl.program_id(0), pl.program_id(1)
    n_tiles = tv_ref.shape[0]
    tm = xb_ref.shape[0]
    slot = lax.rem(n, 2)
    valid = tv_ref[n]

    def fetch(tile, buf_slot):
        rows = pl.ds(pl.multiple_of(tb_ref[tile] * tm, tm), tm)
        return pltpu.make_async_copy(xs_ref.at[rows, :], xbuf.at[buf_slot], sem.at[buf_slot])

    @pl.when(jnp.logical_and(n == 0, f == 0))
    def _():
        fetch(0, 0).start()

    nxt = jnp.minimum(n + 1, n_tiles - 1)

    @pl.when(jnp.logical_and(f == pl.num_programs(1) // 2,
                             jnp.logical_and(n + 1 < n_tiles, tv_ref[nxt] > 0)))
    def _():
        fetch(nxt, 1 - slot).start()

    @pl.when(valid > 0)
    def _():
        @pl.when(f == 0)
        def _():
            fetch(n, slot).wait()
            keep = lax.broadcasted_iota(jnp.int32, xb_ref.shape, 0) < valid
            xb_ref[...] = jnp.where(keep, xbuf[slot], 0.0).astype(BF16)
            ys_ref[...] = jnp.zeros_like(ys_ref)

        xb = xb_ref[...]
        a = _dot(xb, wa_ref[0].astype(BF16))
        b = _dot(xb, wb_ref[0].astype(BF16))
        act = (a * jax.nn.sigmoid(a) * b).astype(BF16)
        ys_ref[...] += _dot(act, wd_ref[0].astype(BF16))


def _experts(tile_e, tile_blk, tile_valid, xs, w_up, w_down):
    n_rows, dp = xs.shape
    tm, tf = TM_MOE, TF_MOE
    nf = D_FF_EXPERT // tf
    nt = tile_e.shape[0]

    def fcol(n, f, tv):
        return jnp.where(tv[n] > 0, f, nf - 1)

    grid_spec = pltpu.PrefetchScalarGridSpec(
        num_scalar_prefetch=3,
        grid=(nt, nf),
        in_specs=[pl.BlockSpec(memory_space=pl.ANY),
                  pl.BlockSpec((1, D_MODEL, tf), lambda n, f, te, tb, tv: (te[n], 0, fcol(n, f, tv))),
                  pl.BlockSpec((1, D_MODEL, tf), lambda n, f, te, tb, tv: (te[n], 0, fcol(n, f, tv) + nf)),
                  pl.BlockSpec((1, tf, D_MODEL), lambda n, f, te, tb, tv: (te[n], fcol(n, f, tv), 0))],
        out_specs=pl.BlockSpec((tm, dp), lambda n, f, te, tb, tv: (tb[n], 0)),
        scratch_shapes=[pltpu.VMEM((2, tm, D_MODEL), F32), pltpu.VMEM((tm, D_MODEL), BF16),
                        pltpu.SemaphoreType.DMA((2,))],
    )
    return pl.pallas_call(
        _experts_kernel,
        grid_spec=grid_spec,
        out_shape=jax.ShapeDtypeStruct((n_rows, dp), F32),
        compiler_params=pltpu.CompilerParams(
            dimension_semantics=("arbitrary", "arbitrary"), vmem_limit_bytes=VMEM_LIMIT),
        name="experts",
    )(tile_e, tile_blk, tile_valid, xs, w_up, w_up, w_down)


def _combine_kernel(pos_ref, next_pos_ref, wts_ref, x_ref, gf_ref, ys_ref, o_ref, y1_ref, y2_ref, sem):
    i = pl.program_id(0)
    tm, d = x_ref.shape
    groups = tm // SUBLANES
    slot = lax.rem(i, 2)

    def gather_tile(rows_ref, buf_slot):
        def issue(g, carry):
            for u in range(SUBLANES):
                r = g * SUBLANES + u
                pltpu.make_async_copy(ys_ref.at[pl.ds(rows_ref[r], 1), :],
                                      y1_ref.at[buf_slot, g, pl.ds(u, 1), :], sem.at[buf_slot]).start(priority=0)
                pltpu.make_async_copy(ys_ref.at[pl.ds(rows_ref[tm + r], 1), :],
                                      y2_ref.at[buf_slot, g, pl.ds(u, 1), :], sem.at[buf_slot]).start(priority=1)
            return carry

        lax.fori_loop(0, groups, issue, 0)

    @pl.when(i == 0)
    def _():
        gather_tile(pos_ref, 0)

    @pl.when(i + 1 < pl.num_programs(0))
    def _():
        gather_tile(next_pos_ref, 1 - slot)

    for _ in range(TOP_K):
        pltpu.make_async_copy(y1_ref.at[slot], y1_ref.at[slot], sem.at[slot]).wait()

    y1 = y1_ref[slot].reshape(tm, d)
    y2 = y2_ref[slot].reshape(tm, d)
    moe = wts_ref[:, 0:1] * y1 + wts_ref[:, 1:2] * y2
    o_ref[...] = _rmsnorm(x_ref[...] + moe, gf_ref[...])


def _combine(pos_flat, wts, x2, g_final, ys):
    T, D = x2.shape
    tm = TM_ROUTE
    last = T // tm - 1
    rows_of = lambda index: pl.BlockSpec((TOP_K * tm,), index, memory_space=pltpu.SMEM)
    return pl.pallas_call(
        _combine_kernel,
        grid=(T // tm,),
        in_specs=[rows_of(lambda i: (i,)), rows_of(lambda i: (jnp.minimum(i + 1, last),)),
                  pl.BlockSpec((tm, LANES), lambda i: (i, 0)),
                  pl.BlockSpec((tm, D), lambda i: (i, 0)),
                  _const_spec(g_final.shape),
                  pl.BlockSpec(memory_space=pl.ANY)],
        out_specs=pl.BlockSpec((tm, D), lambda i: (i, 0)),
        out_shape=jax.ShapeDtypeStruct((T, D), F32),
        scratch_shapes=[pltpu.VMEM((2, tm // SUBLANES, SUBLANES, D), F32),
                        pltpu.VMEM((2, tm // SUBLANES, SUBLANES, D), F32), pltpu.SemaphoreType.DMA((2,))],
        compiler_params=pltpu.CompilerParams(
            dimension_semantics=("arbitrary",), vmem_limit_bytes=VMEM_LIMIT),
        name="combine",
    )(pos_flat, pos_flat, wts, x2, g_final, ys)


def _tile_plan(counts, cap, tm, nt):
    tiles_e = (counts + tm - 1) // tm
    ends = jnp.cumsum(tiles_e)
    total = ends[-1]
    n = jnp.minimum(jnp.arange(nt, dtype=jnp.int32), total - 1)
    e = jnp.minimum(jnp.sum(n[:, None] >= ends[None, :], axis=1).astype(jnp.int32), N_EXPERTS - 1)
    local = n - (ends[e] - tiles_e[e])
    blk = e * (cap // tm) + local
    valid = jnp.where(jnp.arange(nt) < total, jnp.clip(counts[e] - local * tm, 0, tm), 0)
    return e, blk.astype(jnp.int32), valid.astype(jnp.int32)


def _moe_layer(x2, g_ffn, w_router, w_up, w_down, g_final):
    T, D = x2.shape
    cap = T
    wr_pad = jnp.pad(w_router, ((0, 0), (0, LANES - N_EXPERTS)))
    hp, wts, pos, cnt = _router(x2, g_ffn, wr_pad, cap)
    pos_flat = pos[:, :TOP_K].reshape(T // TM_ROUTE, TM_ROUTE, TOP_K).transpose(0, 2, 1).reshape(-1)
    xs = _dispatch(pos_flat, hp, N_EXPERTS * cap)
    nt = TOP_K * T // TM_MOE + N_EXPERTS
    tile_e, tile_blk, tile_valid = _tile_plan(cnt[0, :N_EXPERTS], cap, TM_MOE, nt)
    ys = _experts(tile_e, tile_blk, tile_valid, xs, w_up, w_down)
    return _combine(pos_flat, wts, x2, g_final, ys)


def kernel(x, norm_mix, w_in, b_in, conv_w, conv_b, conv_ln_g, conv_ln_b, conv_out, pool_mix, pool_scale,
           pool_out, sgu_ln_g, sgu_ln_b, sgu_w, sgu_b, sgu_out, w_gate, b_gate, w_o, norm_ffn, ffn_w_up,
           ffn_w_down, moe_router, moe_w_up, moe_w_down, norm_final):
    B, S, D = x.shape
    row = lambda a: a.reshape(1, -1)
    c_conv = 2 * CONV_WIDTH

    for i in range(DEPTH):
        params = {
            "norm": row(norm_mix[i]),
            "w_conv_in": w_in[i, :, :c_conv].astype(BF16), "b_conv_in": row(b_in[i, :c_conv]),
            "w_zg": jnp.concatenate([w_in[i, :, c_conv:], w_gate[i]], axis=1).astype(BF16),
            "b_zg": row(jnp.concatenate([b_in[i, c_conv:], b_gate[i]])),
            "conv_w": jnp.broadcast_to(conv_w[i][:, None, :], (CONV_KERNEL, SUBLANES, CONV_WIDTH)),
            "conv_b": row(conv_b[i]),
            "conv_ln_g": row(conv_ln_g[i]), "conv_ln_b": row(conv_ln_b[i]),
            "conv_out": conv_out[i].astype(BF16),
            "pool_mix": pool_mix[i].astype(BF16), "pool_scale": row(pool_scale[i]),
            "pool_out": pool_out[i].astype(BF16),
            "sgu_ln_g": row(sgu_ln_g[i]), "sgu_ln_b": row(sgu_ln_b[i]), "sgu_w": sgu_w[i],
            "sgu_b": jnp.repeat(sgu_b[i].T, SGU_HEAD_DIM, axis=1),
            "sgu_out": sgu_out[i].astype(BF16),
            "w_o": w_o[i].astype(BF16),
        }
        x = _mixer_layer(x, params)
        x2 = x.reshape(B * S, D)
        if i % 2 == 0:
            x2 = _dense_ffn(x2, row(norm_ffn[i]), ffn_w_up[i // 2].astype(BF16),
                            ffn_w_down[i // 2].astype(BF16))
        else:
            assert i == DEPTH - 1
            x2 = _moe_layer(x2, row(norm_ffn[i]), moe_router[i // 2], moe_w_up[i // 2], moe_w_down[i // 2],
                            row(norm_final))
        x = x2.reshape(B, S, D)
    return x
```

```python
import functools

import jax
import jax.numpy as jnp
from jax import lax
from jax.experimental import pallas as pl
from jax.experimental.pallas import tpu as pltpu

D_MODEL = 1024
DEPTH = 2
CONV_WIDTH = 512
CONV_KERNEL = 31
POOL_WINDOWS = (2, 4, 8, 16)
POOL_GROUP = 128
POOL_WIDTH = POOL_GROUP * len(POOL_WINDOWS)
SGU_HEADS = 4
SGU_HEAD_DIM = 128
SGU_WIDTH = SGU_HEADS * SGU_HEAD_DIM
SGU_BLOCK = 128
N_BRANCH = 3
IN_WIDTH = 2 * CONV_WIDTH + POOL_WIDTH + 2 * SGU_WIDTH
D_FF = 2816
N_EXPERTS = 8
TOP_K = 2
D_FF_EXPERT = 3584
EPS = 1e-6

LANES = 128
SUBLANES = 8
MXU_COLS = 256
N_Z_PIECES = (POOL_WIDTH + 2 * SGU_WIDTH) // MXU_COLS
CONV_HALO = 32
POOL_HALO = 16
VMEM_LIMIT = 56 * 1024 * 1024

TM_MIX = 512
CONV_ROWS = 32
TM_FFN = 512
TM_ROUTE = 512
TM_MOE = 1024
TF_MOE = 512

F32 = jnp.float32
BF16 = jnp.bfloat16


def _const_spec(shape):
    zeros = (0,) * len(shape)
    return pl.BlockSpec(shape, lambda *_: zeros, pipeline_mode=pl.Buffered(1))


def _rmsnorm(x, g):
    return x * lax.rsqrt(jnp.mean(x * x, axis=-1, keepdims=True) + EPS) * g


def _layernorm(x, g, b):
    mu = jnp.mean(x, axis=-1, keepdims=True)
    xc = x - mu
    var = jnp.mean(xc * xc, axis=-1, keepdims=True)
    return xc * lax.rsqrt(var + EPS) * g + b


def _dot(a, b):
    return jnp.dot(a, b, preferred_element_type=F32)


def _dot_split(a, b):
    a_hi = a.astype(BF16)
    a_lo = (a - a_hi.astype(F32)).astype(BF16)
    b_hi = b.astype(BF16)
    b_lo = (b - b_hi.astype(F32)).astype(BF16)
    return _dot(a_hi, b_hi) + (_dot(a_hi, b_lo) + _dot(a_lo, b_hi))


def _mixer_kernel(x_ref, g_ref, wc_ref, bc_ref, wzg_ref, bzg_ref, cw_ref, cb_ref, clg_ref, clb_ref, cout_ref,
                  pmix_ref, pscale_ref, pout_ref, slg_ref, slb_ref, sw_ref, sb_ref, sout_ref, wo_ref,
                  o_ref, hb_buf, conv_ext, conv_sh, pool_ext, zg_buf, act_buf, m_buf):
    tm = x_ref.shape[1]
    j = pl.program_id(1)

    @pl.when(j == 0)
    def _():
        conv_ext[0:CONV_HALO, :] = jnp.zeros((CONV_HALO, CONV_WIDTH), F32)
        pool_ext[0:POOL_HALO, :] = jnp.zeros((POOL_HALO, POOL_WIDTH), F32)

    @pl.when(j > 0)
    def _():
        conv_ext[0:CONV_HALO, :] = conv_ext[tm:tm + CONV_HALO, :]
        pool_ext[0:POOL_HALO, :] = pool_ext[tm:tm + POOL_HALO, :]

    x = x_ref[0]
    hb_buf[...] = _rmsnorm(x, g_ref[...]).astype(BF16)

    def proj(lo, hi):
        return _dot(hb_buf[...], wc_ref[:, lo:hi]) + bc_ref[:, lo:hi]

    conv_ext[CONV_HALO:, :] = proj(0, CONV_WIDTH) * jax.nn.sigmoid(proj(CONV_WIDTH, 2 * CONV_WIDTH))

    n_groups = conv_sh.shape[1]
    for m in range(SUBLANES):
        groups = n_groups - (1 if m else 0)
        conv_sh[m, 0:groups] = conv_ext[m:m + groups * SUBLANES, :].reshape(groups, SUBLANES, CONV_WIDTH)

    base = CONV_HALO - (CONV_KERNEL - 1)
    gpc = CONV_ROWS // SUBLANES

    def conv_chunk(ci):
        acc = jnp.zeros((gpc, SUBLANES, CONV_WIDTH), F32) + cb_ref[...]
        for k in range(CONV_KERNEL):
            q, m = divmod(base + k, SUBLANES)
            acc = acc + conv_sh[m, pl.ds(ci * gpc + q, gpc)] * cw_ref[k]
        hn = _layernorm(acc.reshape(CONV_ROWS, CONV_WIDTH), clg_ref[...], clb_ref[...])
        rows = pl.ds(pl.multiple_of(ci * CONV_ROWS, CONV_ROWS), CONV_ROWS)
        act_buf[rows, :] = (hn * jax.nn.sigmoid(hn)).astype(BF16)

    def piece(c, act):
        cols = slice(c * MXU_COLS, (c + 1) * MXU_COLS)
        zg_buf[c] = act(_dot(hb_buf[...], wzg_ref[:, cols]) + bzg_ref[:, cols])

    n_chunks = tm // CONV_ROWS
    n_pieces = zg_buf.shape[0]
    assert n_chunks <= n_pieces
    for c in range(n_pieces):
        piece(c, (lambda z: z) if c < N_Z_PIECES else jax.nn.sigmoid)
        if c >= n_pieces - n_chunks:
            conv_chunk(c - (n_pieces - n_chunks))

    def z_cols(lo, hi):
        return jnp.concatenate([zg_buf[c] for c in range(lo // MXU_COLS, hi // MXU_COLS)], axis=-1)

    def gate(k):
        c0 = N_Z_PIECES + k * (D_MODEL // MXU_COLS)
        return jnp.concatenate([zg_buf[c] for c in range(c0, c0 + D_MODEL // MXU_COLS)], axis=-1)

    p = z_cols(0, POOL_WIDTH)
    pool_ext[POOL_HALO:, :] = p
    head_pos = lax.broadcasted_iota(jnp.int32, (POOL_HALO, POOL_GROUP), 0)
    diffs = []
    for gi, w in enumerate(POOL_WINDOWS):
        lo, hi = gi * POOL_GROUP, (gi + 1) * POOL_GROUP
        s = pool_ext[:, lo:hi]
        step = 1
        while step < w:
            s = s + pltpu.roll(s, step, axis=0)
            step *= 2
        s = s[POOL_HALO:, :]
        mean = s * (1.0 / w)
        head_cnt = jnp.minimum(head_pos + 1, w).astype(F32)
        head = jnp.where(j == 0, s[:POOL_HALO, :] / head_cnt, mean[:POOL_HALO, :])
        mean = jnp.concatenate([head, mean[POOL_HALO:, :]], axis=0)
        diffs.append((mean - p[:, lo:hi]).astype(BF16))
    u = jax.nn.gelu(z_cols(POOL_WIDTH, POOL_WIDTH + SGU_WIDTH), approximate=True)
    mixed = [_dot(diffs[gi], pmix_ref[gi]) for gi in range(len(POOL_WINDOWS))]
    yb = jnp.concatenate(mixed, axis=-1) * pscale_ref[...]
    v = jax.nn.gelu(z_cols(POOL_WIDTH + SGU_WIDTH, POOL_WIDTH + 2 * SGU_WIDTH), approximate=True)
    vn = _layernorm(v, slg_ref[...], slb_ref[...]).astype(BF16)
    m_buf[...] = gate(1) * _dot(yb.astype(BF16), pout_ref[...])

    row = lax.broadcasted_iota(jnp.int32, (SGU_BLOCK, SGU_BLOCK), 0)
    col = lax.broadcasted_iota(jnp.int32, (SGU_BLOCK, SGU_BLOCK), 1)
    ws = [jnp.where(row >= col, sw_ref[hd], 0.0).astype(BF16) for hd in range(SGU_HEADS)]
    blocks = []
    for r in range(0, tm, SGU_BLOCK):
        heads = [_dot(ws[hd], vn[r:r + SGU_BLOCK, hd * SGU_HEAD_DIM:(hd + 1) * SGU_HEAD_DIM])
                 for hd in range(SGU_HEADS)]
        blocks.append(jnp.concatenate(heads, axis=-1) + sb_ref[...])
    y = u * jnp.concatenate(blocks, axis=0)
    m_buf[...] += gate(2) * _dot(y.astype(BF16), sout_ref[...])
    merged = m_buf[...] + gate(0) * _dot(act_buf[...], cout_ref[...])

    o_ref[0] = x + _dot(merged.astype(BF16), wo_ref[...])


def _mixer_layer(x, p):
    B, S, D = x.shape
    tm = TM_MIX
    consts = [p["norm"], p["w_conv_in"], p["b_conv_in"], p["w_zg"], p["b_zg"], p["conv_w"], p["conv_b"],
              p["conv_ln_g"], p["conv_ln_b"], p["conv_out"], p["pool_mix"], p["pool_scale"], p["pool_out"],
              p["sgu_ln_g"], p["sgu_ln_b"], p["sgu_w"], p["sgu_b"], p["sgu_out"], p["w_o"]]
    x_spec = pl.BlockSpec((1, tm, D), lambda b, j: (b, j, 0))
    n_pieces = p["w_zg"].shape[1] // MXU_COLS
    return pl.pallas_call(
        _mixer_kernel,
        grid=(B, S // tm),
        in_specs=[x_spec] + [_const_spec(c.shape) for c in consts],
        out_specs=x_spec,
        out_shape=jax.ShapeDtypeStruct(x.shape, F32),
        scratch_shapes=[pltpu.VMEM((tm, D), BF16),
                        pltpu.VMEM((tm + CONV_HALO, CONV_WIDTH), F32),
                        pltpu.VMEM((SUBLANES, (tm + CONV_HALO) // SUBLANES, SUBLANES, CONV_WIDTH), F32),
                        pltpu.VMEM((tm + POOL_HALO, POOL_WIDTH), F32),
                        pltpu.VMEM((n_pieces, tm, MXU_COLS), F32),
                        pltpu.VMEM((tm, CONV_WIDTH), BF16),
                        pltpu.VMEM((tm, D), F32)],
        compiler_params=pltpu.CompilerParams(
            dimension_semantics=("arbitrary", "arbitrary"), vmem_limit_bytes=VMEM_LIMIT),
        name="mixer",
    )(x, *consts)


def _ffn_kernel(x_ref, g_ref, wup_ref, wdn_ref, o_ref):
    x = x_ref[...]
    hb = _rmsnorm(x, g_ref[...]).astype(BF16)
    a = _dot(hb, wup_ref[:, :D_FF])
    b = _dot(hb, wup_ref[:, D_FF:])
    act = (a * jax.nn.sigmoid(a) * b).astype(BF16)
    o_ref[...] = x + _dot(act, wdn_ref[...])


def _dense_ffn(x2, g, w_up, w_down):
    T, D = x2.shape
    tm = TM_FFN
    x_spec = pl.BlockSpec((tm, D), lambda i: (i, 0))
    return pl.pallas_call(
        _ffn_kernel,
        grid=(T // tm,),
        in_specs=[x_spec, _const_spec(g.shape), _const_spec(w_up.shape), _const_spec(w_down.shape)],
        out_specs=x_spec,
        out_shape=jax.ShapeDtypeStruct(x2.shape, F32),
        compiler_params=pltpu.CompilerParams(
            dimension_semantics=("arbitrary",), vmem_limit_bytes=VMEM_LIMIT),
        name="dense_ffn",
    )(x2, g, w_up, w_down)


def _router_kernel(x_ref, g_ref, wr_ref, hp_ref, wts_ref, pos_ref, cnt_ref, carry_ref, *, cap):
    tm = x_ref.shape[0]

    @pl.when(pl.program_id(0) == 0)
    def _():
        carry_ref[...] = jnp.zeros_like(carry_ref)

    h = _rmsnorm(x_ref[...], g_ref[...])
    hp_ref[...] = h
    logits = _dot_split(h, wr_ref[...])
    lane = lax.broadcasted_iota(jnp.int32, logits.shape, 1)
    neg = jnp.float32(-jnp.inf)
    logits = jnp.where(lane < N_EXPERTS, logits, neg)
    m1 = jnp.max(logits, axis=-1, keepdims=True)
    i1 = jnp.min(jnp.where(logits == m1, lane, LANES), axis=-1, keepdims=True)
    rest = jnp.where(lane == i1, neg, logits)
    m2 = jnp.max(rest, axis=-1, keepdims=True)
    i2 = jnp.min(jnp.where(rest == m2, lane, LANES), axis=-1, keepdims=True)
    e2 = jnp.exp(m2 - m1)
    w1 = 1.0 / (1.0 + e2)
    w2 = e2 / (1.0 + e2)

    chosen = jnp.where(jnp.logical_or(lane == i1, lane == i2), 1.0, 0.0)
    r_io = lax.broadcasted_iota(jnp.int32, (tm, tm), 0)
    c_io = lax.broadcasted_iota(jnp.int32, (tm, tm), 1)
    tri = jnp.where(r_io >= c_io, 1.0, 0.0).astype(BF16)
    incl = _dot(tri, chosen.astype(BF16))
    before = incl - chosen + carry_ref[...]
    rank1 = jnp.sum(jnp.where(lane == i1, before, 0.0), axis=-1, keepdims=True).astype(jnp.int32)
    rank2 = jnp.sum(jnp.where(lane == i2, before, 0.0), axis=-1, keepdims=True).astype(jnp.int32)
    carry_ref[...] = carry_ref[...] + incl[tm - 1:tm, :]
    cnt_ref[...] = carry_ref[...].astype(jnp.int32)

    wts_ref[...] = jnp.where(lane == 0, w1, jnp.where(lane == 1, w2, 0.0))
    pos_ref[...] = jnp.where(lane == 0, i1 * cap + rank1, jnp.where(lane == 1, i2 * cap + rank2, 0))


def _router(x2, g, wr_pad, cap):
    T, D = x2.shape
    tm = TM_ROUTE
    tile = lambda w: pl.BlockSpec((tm, w), lambda i: (i, 0))
    return pl.pallas_call(
        functools.partial(_router_kernel, cap=cap),
        grid=(T // tm,),
        in_specs=[tile(D), _const_spec(g.shape), _const_spec(wr_pad.shape)],
        out_specs=[tile(D), tile(LANES), tile(LANES), pl.BlockSpec((1, LANES), lambda i: (0, 0))],
        out_shape=[jax.ShapeDtypeStruct((T, D), F32), jax.ShapeDtypeStruct((T, LANES), F32),
                   jax.ShapeDtypeStruct((T, LANES), jnp.int32), jax.ShapeDtypeStruct((1, LANES), jnp.int32)],
        scratch_shapes=[pltpu.VMEM((1, LANES), F32)],
        compiler_params=pltpu.CompilerParams(
            dimension_semantics=("arbitrary",), vmem_limit_bytes=VMEM_LIMIT),
        name="router",
    )(x2, g, wr_pad)


def _dispatch_kernel(pos_ref, h_ref, xs_ref, hbuf, fetch_sem, row_sem):
    i = pl.program_id(0)
    n_slots, groups = hbuf.shape[0], hbuf.shape[1]
    tm = groups * SUBLANES
    slot = lax.rem(i, n_slots)
    next_slot = lax.rem(i + 1, n_slots)
    prev_slot = lax.rem(i + n_slots - 1, n_slots)

    def fetch(tile, s):
        return pltpu.make_async_copy(h_ref.at[pl.ds(tile * groups, groups)], hbuf.at[s], fetch_sem.at[s])

    def await_rows(s):
        for _ in range(TOP_K):
            pltpu.make_async_copy(hbuf.at[s], hbuf.at[s], row_sem.at[s]).wait()

    @pl.when(i == 0)
    def _():
        fetch(0, 0).start()

    @pl.when(i + 1 < pl.num_programs(0))
    def _():
        fetch(i + 1, next_slot).start()

    fetch(i, slot).wait()

    def issue(g, carry):
        for u in range(SUBLANES):
            r = g * SUBLANES + u
            src = hbuf.at[slot, g, pl.ds(u, 1), :]
            pltpu.make_async_copy(src, xs_ref.at[pl.ds(pos_ref[r], 1), :], row_sem.at[slot]).start(priority=0)
            pltpu.make_async_copy(src, xs_ref.at[pl.ds(pos_ref[tm + r], 1), :], row_sem.at[slot]).start(priority=1)
        return carry

    lax.fori_loop(0, groups, issue, 0)

    @pl.when(i >= 1)
    def _():
        await_rows(prev_slot)

    @pl.when(i == pl.num_programs(0) - 1)
    def _():
        await_rows(slot)


def _dispatch(pos_flat, hp, n_rows):
    T, dp = hp.shape
    tm = TM_ROUTE
    return pl.pallas_call(
        _dispatch_kernel,
        grid=(T // tm,),
        in_specs=[pl.BlockSpec((TOP_K * tm,), lambda i: (i,), memory_space=pltpu.SMEM),
                  pl.BlockSpec(memory_space=pl.ANY)],
        out_specs=pl.BlockSpec(memory_space=pl.ANY),
        out_shape=jax.ShapeDtypeStruct((n_rows, dp), F32),
        scratch_shapes=[pltpu.VMEM((3, tm // SUBLANES, SUBLANES, dp), F32),
                        pltpu.SemaphoreType.DMA((3,)), pltpu.SemaphoreType.DMA((3,))],
        compiler_params=pltpu.CompilerParams(
            dimension_semantics=("arbitrary",), vmem_limit_bytes=VMEM_LIMIT, has_side_effects=True),
        name="dispatch",
    )(pos_flat, hp.reshape(T // SUBLANES, SUBLANES, dp))


def _experts_kernel(te_ref, tb_ref, tv_ref, xs_ref, wa_ref, wb_ref, wd_ref, ys_ref, xbuf, xb_ref, sem):
    n, f = pl.program_id(0), pl.program_id(1)
    n_tiles = tv_ref.shape[0]
    tm = xb_ref.shape[0]
    slot = lax.rem(n, 2)
    valid = tv_ref[n]

    def fetch(tile, buf_slot):
        rows = pl.ds(pl.multiple_of(tb_ref[tile] * tm, tm), tm)
        return pltpu.make_async_copy(xs_ref.at[rows, :], xbuf.at[buf_slot], sem.at[buf_slot])

    @pl.when(jnp.logical_and(n == 0, f == 0))
    def _():
        fetch(0, 0).start()

    nxt = jnp.minimum(n + 1, n_tiles - 1)

    @pl.when(jnp.logical_and(f == pl.num_programs(1) // 2,
                             jnp.logical_and(n + 1 < n_tiles, tv_ref[nxt] > 0)))
    def _():
        fetch(nxt, 1 - slot).start()

    def hidden_columns_step(first):
        xb = xb_ref[...]
        a = _dot(xb, wa_ref[0].astype(BF16))
        b = _dot(xb, wb_ref[0].astype(BF16))
        act = (a * jax.nn.sigmoid(a) * b).astype(BF16)
        down = _dot(act, wd_ref[0].astype(BF16))
        if first:
            ys_ref[...] = down
        else:
            ys_ref[...] += down

    @pl.when(jnp.logical_and(valid > 0, f == 0))
    def _():
        fetch(n, slot).wait()
        keep = lax.broadcasted_iota(jnp.int32, xb_ref.shape, 0) < valid
        xb_ref[...] = jnp.where(keep, xbuf[slot], 0.0).astype(BF16)
        hidden_columns_step(first=True)

    @pl.when(jnp.logical_and(valid > 0, f > 0))
    def _():
        hidden_columns_step(first=False)


def _experts(tile_e, tile_blk, tile_valid, xs, w_up, w_down):
    n_rows, dp = xs.shape
    tm, tf = TM_MOE, TF_MOE
    nf = D_FF_EXPERT // tf
    nt = tile_e.shape[0]

    def fcol(n, f, tv):
        return jnp.where(tv[n] > 0, f, nf - 1)

    grid_spec = pltpu.PrefetchScalarGridSpec(
        num_scalar_prefetch=3,
        grid=(nt, nf),
        in_specs=[pl.BlockSpec(memory_space=pl.ANY),
                  pl.BlockSpec((1, D_MODEL, tf), lambda n, f, te, tb, tv: (te[n], 0, fcol(n, f, tv))),
                  pl.BlockSpec((1, D_MODEL, tf), lambda n, f, te, tb, tv: (te[n], 0, fcol(n, f, tv) + nf)),
                  pl.BlockSpec((1, tf, D_MODEL), lambda n, f, te, tb, tv: (te[n], fcol(n, f, tv), 0))],
        out_specs=pl.BlockSpec((tm, dp), lambda n, f, te, tb, tv: (tb[n], 0)),
        scratch_shapes=[pltpu.VMEM((2, tm, D_MODEL), F32), pltpu.VMEM((tm, D_MODEL), BF16),
                        pltpu.SemaphoreType.DMA((2,))],
    )
    return pl.pallas_call(
        _experts_kernel,
        grid_spec=grid_spec,
        out_shape=jax.ShapeDtypeStruct((n_rows, dp), F32),
        compiler_params=pltpu.CompilerParams(
            dimension_semantics=("arbitrary", "arbitrary"), vmem_limit_bytes=VMEM_LIMIT),
        name="experts",
    )(tile_e, tile_blk, tile_valid, xs, w_up, w_up, w_down)


def _combine_kernel(pos_ref, next_pos_ref, wts_ref, x_ref, gf_ref, ys_ref, o_ref, y1_ref, y2_ref, sem):
    i = pl.program_id(0)
    tm, d = x_ref.shape
    groups = tm // SUBLANES
    slot = lax.rem(i, 2)

    def gather_tile(rows_ref, buf_slot):
        def issue(g, carry):
            for u in range(SUBLANES):
                r = g * SUBLANES + u
                pltpu.make_async_copy(ys_ref.at[pl.ds(rows_ref[r], 1), :],
                                      y1_ref.at[buf_slot, g, pl.ds(u, 1), :], sem.at[buf_slot]).start(priority=0)
                pltpu.make_async_copy(ys_ref.at[pl.ds(rows_ref[tm + r], 1), :],
                                      y2_ref.at[buf_slot, g, pl.ds(u, 1), :], sem.at[buf_slot]).start(priority=1)
            return carry

        lax.fori_loop(0, groups, issue, 0)

    @pl.when(i == 0)
    def _():
        gather_tile(pos_ref, 0)

    @pl.when(i + 1 < pl.num_programs(0))
    def _():
        gather_tile(next_pos_ref, 1 - slot)

    for _ in range(TOP_K):
        pltpu.make_async_copy(y1_ref.at[slot], y1_ref.at[slot], sem.at[slot]).wait()

    y1 = y1_ref[slot].reshape(tm, d)
    y2 = y2_ref[slot].reshape(tm, d)
    moe = wts_ref[:, 0:1] * y1 + wts_ref[:, 1:2] * y2
    o_ref[...] = _rmsnorm(x_ref[...] + moe, gf_ref[...])


def _combine(pos_flat, wts, x2, g_final, ys):
    T, D = x2.shape
    tm = TM_ROUTE
    last = T // tm - 1
    rows_of = lambda index: pl.BlockSpec((TOP_K * tm,), index, memory_space=pltpu.SMEM)
    return pl.pallas_call(
        _combine_kernel,
        grid=(T // tm,),
        in_specs=[rows_of(lambda i: (i,)), rows_of(lambda i: (jnp.minimum(i + 1, last),)),
                  pl.BlockSpec((tm, LANES), lambda i: (i, 0)),
                  pl.BlockSpec((tm, D), lambda i: (i, 0)),
                  _const_spec(g_final.shape),
                  pl.BlockSpec(memory_space=pl.ANY)],
        out_specs=pl.BlockSpec((tm, D), lambda i: (i, 0)),
        out_shape=jax.ShapeDtypeStruct((T, D), F32),
        scratch_shapes=[pltpu.VMEM((2, tm // SUBLANES, SUBLANES, D), F32),
                        pltpu.VMEM((2, tm // SUBLANES, SUBLANES, D), F32), pltpu.SemaphoreType.DMA((2,))],
        compiler_params=pltpu.CompilerParams(
            dimension_semantics=("arbitrary",), vmem_limit_bytes=VMEM_LIMIT),
        name="combine",
    )(pos_flat, pos_flat, wts, x2, g_final, ys)


def _tile_plan(counts, cap, tm, nt):
    tiles_e = (counts + tm - 1) // tm
    ends = jnp.cumsum(tiles_e)
    total = ends[-1]
    n = jnp.minimum(jnp.arange(nt, dtype=jnp.int32), total - 1)
    e = jnp.minimum(jnp.sum(n[:, None] >= ends[None, :], axis=1).astype(jnp.int32), N_EXPERTS - 1)
    local = n - (ends[e] - tiles_e[e])
    blk = e * (cap // tm) + local
    valid = jnp.where(jnp.arange(nt) < total, jnp.clip(counts[e] - local * tm, 0, tm), 0)
    return e, blk.astype(jnp.int32), valid.astype(jnp.int32)


def _moe_layer(x2, g_ffn, w_router, w_up, w_down, g_final):
    T, D = x2.shape
    cap = T
    wr_pad = jnp.pad(w_router, ((0, 0), (0, LANES - N_EXPERTS)))
    hp, wts, pos, cnt = _router(x2, g_ffn, wr_pad, cap)
    pos_flat = pos[:, :TOP_K].reshape(T // TM_ROUTE, TM_ROUTE, TOP_K).transpose(0, 2, 1).reshape(-1)
    xs = _dispatch(pos_flat, hp, N_EXPERTS * cap)
    nt = TOP_K * T // TM_MOE + N_EXPERTS
    tile_e, tile_blk, tile_valid = _tile_plan(cnt[0, :N_EXPERTS], cap, TM_MOE, nt)
    ys = _experts(tile_e, tile_blk, tile_valid, xs, w_up, w_down)
    return _combine(pos_flat, wts, x2, g_final, ys)


def kernel(x, norm_mix, w_in, b_in, conv_w, conv_b, conv_ln_g, conv_ln_b, conv_out, pool_mix, pool_scale,
           pool_out, sgu_ln_g, sgu_ln_b, sgu_w, sgu_b, sgu_out, w_gate, b_gate, w_o, norm_ffn, ffn_w_up,
           ffn_w_down, moe_router, moe_w_up, moe_w_down, norm_final):
    B, S, D = x.shape
    row = lambda a: a.reshape(1, -1)
    c_conv = 2 * CONV_WIDTH

    for i in range(DEPTH):
        params = {
            "norm": row(norm_mix[i]),
            "w_conv_in": w_in[i, :, :c_conv].astype(BF16), "b_conv_in": row(b_in[i, :c_conv]),
            "w_zg": jnp.concatenate([w_in[i, :, c_conv:], w_gate[i]], axis=1).astype(BF16),
            "b_zg": row(jnp.concatenate([b_in[i, c_conv:], b_gate[i]])),
            "conv_w": jnp.broadcast_to(conv_w[i][:, None, :], (CONV_KERNEL, SUBLANES, CONV_WIDTH)),
            "conv_b": row(conv_b[i]),
            "conv_ln_g": row(conv_ln_g[i]), "conv_ln_b": row(conv_ln_b[i]),
            "conv_out": conv_out[i].astype(BF16),
            "pool_mix": pool_mix[i].astype(BF16), "pool_scale": row(pool_scale[i]),
            "pool_out": pool_out[i].astype(BF16),
            "sgu_ln_g": row(sgu_ln_g[i]), "sgu_ln_b": row(sgu_ln_b[i]), "sgu_w": sgu_w[i],
            "sgu_b": jnp.repeat(sgu_b[i].T, SGU_HEAD_DIM, axis=1),
            "sgu_out": sgu_out[i].astype(BF16),
            "w_o": w_o[i].astype(BF16),
        }
        x = _mixer_layer(x, params)
        x2 = x.reshape(B * S, D)
        if i % 2 == 0:
            x2 = _dense_ffn(x2, row(norm_ffn[i]), ffn_w_up[i // 2].astype(BF16),
                            ffn_w_down[i // 2].astype(BF16))
        else:
            assert i == DEPTH - 1
            x2 = _moe_layer(x2, row(norm_ffn[i]), moe_router[i // 2], moe_w_up[i // 2], moe_w_down[i // 2],
                            row(norm_final))
        x = x2.reshape(B, S, D)
    return x
```

```python
import functools

import jax
import jax.numpy as jnp
from jax import lax
from jax.experimental import pallas as pl
from jax.experimental.pallas import tpu as pltpu

D_MODEL = 1024
DEPTH = 2
CONV_WIDTH = 512
CONV_KERNEL = 31
POOL_WINDOWS = (2, 4, 8, 16)
POOL_GROUP = 128
POOL_WIDTH = POOL_GROUP * len(POOL_WINDOWS)
SGU_HEADS = 4
SGU_HEAD_DIM = 128
SGU_WIDTH = SGU_HEADS * SGU_HEAD_DIM
SGU_BLOCK = 128
N_BRANCH = 3
IN_WIDTH = 2 * CONV_WIDTH + POOL_WIDTH + 2 * SGU_WIDTH
D_FF = 2816
N_EXPERTS = 8
TOP_K = 2
D_FF_EXPERT = 3584
EPS = 1e-6

LANES = 128
SUBLANES = 8
MXU_COLS = 256
N_Z_PIECES = (POOL_WIDTH + 2 * SGU_WIDTH) // MXU_COLS
CONV_HALO = 32
POOL_HALO = 16
VMEM_LIMIT = 56 * 1024 * 1024

TM_MIX = 512
CONV_ROWS = 32
TM_FFN = 512
TM_ROUTE = 512
TM_ROWS = 1024
TM_MOE = 1024
TF_MOE = 512

F32 = jnp.float32
BF16 = jnp.bfloat16


def _const_spec(shape):
    zeros = (0,) * len(shape)
    return pl.BlockSpec(shape, lambda *_: zeros, pipeline_mode=pl.Buffered(1))


def _rmsnorm(x, g):
    return x * lax.rsqrt(jnp.mean(x * x, axis=-1, keepdims=True) + EPS) * g


def _layernorm(x, g, b):
    mu = jnp.mean(x, axis=-1, keepdims=True)
    xc = x - mu
    var = jnp.mean(xc * xc, axis=-1, keepdims=True)
    return xc * lax.rsqrt(var + EPS) * g + b


def _dot(a, b):
    return jnp.dot(a, b, preferred_element_type=F32)


def _dot_split(a, b):
    a_hi = a.astype(BF16)
    a_lo = (a - a_hi.astype(F32)).astype(BF16)
    b_hi = b.astype(BF16)
    b_lo = (b - b_hi.astype(F32)).astype(BF16)
    return _dot(a_hi, b_hi) + (_dot(a_hi, b_lo) + _dot(a_lo, b_hi))


def _mixer_kernel(x_ref, g_ref, wc_ref, bc_ref, wzg_ref, bzg_ref, cw_ref, cb_ref, clg_ref, clb_ref, cout_ref,
                  pmix_ref, pscale_ref, pout_ref, slg_ref, slb_ref, sw_ref, sb_ref, sout_ref, wo_ref,
                  o_ref, hb_buf, conv_ext, conv_sh, pool_ext, zg_buf, act_buf, m_buf):
    tm = x_ref.shape[1]
    j = pl.program_id(1)

    @pl.when(j == 0)
    def _():
        conv_ext[0:CONV_HALO, :] = jnp.zeros((CONV_HALO, CONV_WIDTH), F32)
        pool_ext[0:POOL_HALO, :] = jnp.zeros((POOL_HALO, POOL_WIDTH), F32)

    @pl.when(j > 0)
    def _():
        conv_ext[0:CONV_HALO, :] = conv_ext[tm:tm + CONV_HALO, :]
        pool_ext[0:POOL_HALO, :] = pool_ext[tm:tm + POOL_HALO, :]

    x = x_ref[0]
    hb_buf[...] = _rmsnorm(x, g_ref[...]).astype(BF16)

    def proj(lo, hi):
        return _dot(hb_buf[...], wc_ref[:, lo:hi]) + bc_ref[:, lo:hi]

    conv_ext[CONV_HALO:, :] = proj(0, CONV_WIDTH) * jax.nn.sigmoid(proj(CONV_WIDTH, 2 * CONV_WIDTH))

    n_groups = conv_sh.shape[1]
    for m in range(SUBLANES):
        groups = n_groups - (1 if m else 0)
        conv_sh[m, 0:groups] = conv_ext[m:m + groups * SUBLANES, :].reshape(groups, SUBLANES, CONV_WIDTH)

    base = CONV_HALO - (CONV_KERNEL - 1)
    gpc = CONV_ROWS // SUBLANES

    def conv_chunk(ci):
        acc = jnp.zeros((gpc, SUBLANES, CONV_WIDTH), F32) + cb_ref[...]
        for k in range(CONV_KERNEL):
            q, m = divmod(base + k, SUBLANES)
            acc = acc + conv_sh[m, pl.ds(ci * gpc + q, gpc)] * cw_ref[k]
        hn = _layernorm(acc.reshape(CONV_ROWS, CONV_WIDTH), clg_ref[...], clb_ref[...])
        rows = pl.ds(pl.multiple_of(ci * CONV_ROWS, CONV_ROWS), CONV_ROWS)
        act_buf[rows, :] = (hn * jax.nn.sigmoid(hn)).astype(BF16)

    def piece(c, act):
        cols = slice(c * MXU_COLS, (c + 1) * MXU_COLS)
        zg_buf[c] = act(_dot(hb_buf[...], wzg_ref[:, cols]) + bzg_ref[:, cols])

    n_chunks = tm // CONV_ROWS
    n_pieces = zg_buf.shape[0]
    assert n_chunks <= n_pieces
    for c in range(n_pieces):
        piece(c, (lambda z: z) if c < N_Z_PIECES else jax.nn.sigmoid)
        if c >= n_pieces - n_chunks:
            conv_chunk(c - (n_pieces - n_chunks))

    def z_cols(lo, hi):
        return jnp.concatenate([zg_buf[c] for c in range(lo // MXU_COLS, hi // MXU_COLS)], axis=-1)

    def gate(k):
        c0 = N_Z_PIECES + k * (D_MODEL // MXU_COLS)
        return jnp.concatenate([zg_buf[c] for c in range(c0, c0 + D_MODEL // MXU_COLS)], axis=-1)

    p = z_cols(0, POOL_WIDTH)
    pool_ext[POOL_HALO:, :] = p
    head_pos = lax.broadcasted_iota(jnp.int32, (POOL_HALO, POOL_GROUP), 0)
    diffs = []
    for gi, w in enumerate(POOL_WINDOWS):
        lo, hi = gi * POOL_GROUP, (gi + 1) * POOL_GROUP
        s = pool_ext[:, lo:hi]
        step = 1
        while step < w:
            s = s + pltpu.roll(s, step, axis=0)
            step *= 2
        s = s[POOL_HALO:, :]
        mean = s * (1.0 / w)
        head_cnt = jnp.minimum(head_pos + 1, w).astype(F32)
        head = jnp.where(j == 0, s[:POOL_HALO, :] / head_cnt, mean[:POOL_HALO, :])
        mean = jnp.concatenate([head, mean[POOL_HALO:, :]], axis=0)
        diffs.append((mean - p[:, lo:hi]).astype(BF16))
    u = jax.nn.gelu(z_cols(POOL_WIDTH, POOL_WIDTH + SGU_WIDTH), approximate=True)
    mixed = [_dot(diffs[gi], pmix_ref[gi]) for gi in range(len(POOL_WINDOWS))]
    yb = jnp.concatenate(mixed, axis=-1) * pscale_ref[...]
    v = jax.nn.gelu(z_cols(POOL_WIDTH + SGU_WIDTH, POOL_WIDTH + 2 * SGU_WIDTH), approximate=True)
    vn = _layernorm(v, slg_ref[...], slb_ref[...]).astype(BF16)
    m_buf[...] = gate(1) * _dot(yb.astype(BF16), pout_ref[...])

    row = lax.broadcasted_iota(jnp.int32, (SGU_BLOCK, SGU_BLOCK), 0)
    col = lax.broadcasted_iota(jnp.int32, (SGU_BLOCK, SGU_BLOCK), 1)
    ws = [jnp.where(row >= col, sw_ref[hd], 0.0).astype(BF16) for hd in range(SGU_HEADS)]
    blocks = []
    for r in range(0, tm, SGU_BLOCK):
        heads = [_dot(ws[hd], vn[r:r + SGU_BLOCK, hd * SGU_HEAD_DIM:(hd + 1) * SGU_HEAD_DIM])
                 for hd in range(SGU_HEADS)]
        blocks.append(jnp.concatenate(heads, axis=-1) + sb_ref[...])
    y = u * jnp.concatenate(blocks, axis=0)
    m_buf[...] += gate(2) * _dot(y.astype(BF16), sout_ref[...])
    merged = m_buf[...] + gate(0) * _dot(act_buf[...], cout_ref[...])

    o_ref[0] = x + _dot(merged.astype(BF16), wo_ref[...])


def _mixer_layer(x, p):
    B, S, D = x.shape
    tm = TM_MIX
    consts = [p["norm"], p["w_conv_in"], p["b_conv_in"], p["w_zg"], p["b_zg"], p["conv_w"], p["conv_b"],
              p["conv_ln_g"], p["conv_ln_b"], p["conv_out"], p["pool_mix"], p["pool_scale"], p["pool_out"],
              p["sgu_ln_g"], p["sgu_ln_b"], p["sgu_w"], p["sgu_b"], p["sgu_out"], p["w_o"]]
    x_spec = pl.BlockSpec((1, tm, D), lambda b, j: (b, j, 0))
    n_pieces = p["w_zg"].shape[1] // MXU_COLS
    return pl.pallas_call(
        _mixer_kernel,
        grid=(B, S // tm),
        in_specs=[x_spec] + [_const_spec(c.shape) for c in consts],
        out_specs=x_spec,
        out_shape=jax.ShapeDtypeStruct(x.shape, F32),
        scratch_shapes=[pltpu.VMEM((tm, D), BF16),
                        pltpu.VMEM((tm + CONV_HALO, CONV_WIDTH), F32),
                        pltpu.VMEM((SUBLANES, (tm + CONV_HALO) // SUBLANES, SUBLANES, CONV_WIDTH), F32),
                        pltpu.VMEM((tm + POOL_HALO, POOL_WIDTH), F32),
                        pltpu.VMEM((n_pieces, tm, MXU_COLS), F32),
                        pltpu.VMEM((tm, CONV_WIDTH), BF16),
                        pltpu.VMEM((tm, D), F32)],
        compiler_params=pltpu.CompilerParams(
            dimension_semantics=("arbitrary", "arbitrary"), vmem_limit_bytes=VMEM_LIMIT),
        name="mixer",
    )(x, *consts)


def _ffn_kernel(x_ref, g_ref, wup_ref, wdn_ref, o_ref):
    x = x_ref[...]
    hb = _rmsnorm(x, g_ref[...]).astype(BF16)
    a = _dot(hb, wup_ref[:, :D_FF])
    b = _dot(hb, wup_ref[:, D_FF:])
    act = (a * jax.nn.sigmoid(a) * b).astype(BF16)
    o_ref[...] = x + _dot(act, wdn_ref[...])


def _dense_ffn(x2, g, w_up, w_down):
    T, D = x2.shape
    tm = TM_FFN
    x_spec = pl.BlockSpec((tm, D), lambda i: (i, 0))
    return pl.pallas_call(
        _ffn_kernel,
        grid=(T // tm,),
        in_specs=[x_spec, _const_spec(g.shape), _const_spec(w_up.shape), _const_spec(w_down.shape)],
        out_specs=x_spec,
        out_shape=jax.ShapeDtypeStruct(x2.shape, F32),
        compiler_params=pltpu.CompilerParams(
            dimension_semantics=("arbitrary",), vmem_limit_bytes=VMEM_LIMIT),
        name="dense_ffn",
    )(x2, g, w_up, w_down)


def _router_kernel(x_ref, g_ref, wr_ref, hp_ref, wts_ref, pos_ref, cnt_ref, carry_ref, *, cap):
    tm = x_ref.shape[0]

    @pl.when(pl.program_id(0) == 0)
    def _():
        carry_ref[...] = jnp.zeros_like(carry_ref)

    h = _rmsnorm(x_ref[...], g_ref[...])
    hp_ref[...] = h
    logits = _dot_split(h, wr_ref[...])
    lane = lax.broadcasted_iota(jnp.int32, logits.shape, 1)
    neg = jnp.float32(-jnp.inf)
    logits = jnp.where(lane < N_EXPERTS, logits, neg)
    m1 = jnp.max(logits, axis=-1, keepdims=True)
    i1 = jnp.min(jnp.where(logits == m1, lane, LANES), axis=-1, keepdims=True)
    rest = jnp.where(lane == i1, neg, logits)
    m2 = jnp.max(rest, axis=-1, keepdims=True)
    i2 = jnp.min(jnp.where(rest == m2, lane, LANES), axis=-1, keepdims=True)
    e2 = jnp.exp(m2 - m1)
    w1 = 1.0 / (1.0 + e2)
    w2 = e2 / (1.0 + e2)

    chosen = jnp.where(jnp.logical_or(lane == i1, lane == i2), 1.0, 0.0)
    r_io = lax.broadcasted_iota(jnp.int32, (tm, tm), 0)
    c_io = lax.broadcasted_iota(jnp.int32, (tm, tm), 1)
    tri = jnp.where(r_io >= c_io, 1.0, 0.0).astype(BF16)
    incl = _dot(tri, chosen.astype(BF16))
    before = incl - chosen + carry_ref[...]
    rank1 = jnp.sum(jnp.where(lane == i1, before, 0.0), axis=-1, keepdims=True).astype(jnp.int32)
    rank2 = jnp.sum(jnp.where(lane == i2, before, 0.0), axis=-1, keepdims=True).astype(jnp.int32)
    carry_ref[...] = carry_ref[...] + incl[tm - 1:tm, :]
    cnt_ref[...] = carry_ref[...].astype(jnp.int32)

    wts_ref[...] = jnp.where(lane == 0, w1, jnp.where(lane == 1, w2, 0.0))
    pos_ref[...] = jnp.where(lane == 0, i1 * cap + rank1, jnp.where(lane == 1, i2 * cap + rank2, 0))


def _router(x2, g, wr_pad, cap):
    T, D = x2.shape
    tm = TM_ROUTE
    tile = lambda w: pl.BlockSpec((tm, w), lambda i: (i, 0))
    return pl.pallas_call(
        functools.partial(_router_kernel, cap=cap),
        grid=(T // tm,),
        in_specs=[tile(D), _const_spec(g.shape), _const_spec(wr_pad.shape)],
        out_specs=[tile(D), tile(LANES), tile(LANES), pl.BlockSpec((1, LANES), lambda i: (0, 0))],
        out_shape=[jax.ShapeDtypeStruct((T, D), F32), jax.ShapeDtypeStruct((T, LANES), F32),
                   jax.ShapeDtypeStruct((T, LANES), jnp.int32), jax.ShapeDtypeStruct((1, LANES), jnp.int32)],
        scratch_shapes=[pltpu.VMEM((1, LANES), F32)],
        compiler_params=pltpu.CompilerParams(
            dimension_semantics=("arbitrary",), vmem_limit_bytes=VMEM_LIMIT),
        name="router",
    )(x2, g, wr_pad)


def _dispatch_kernel(pos_ref, h_ref, xs_ref, hbuf, fetch_sem, row_sem):
    i = pl.program_id(0)
    n_slots, groups = hbuf.shape[0], hbuf.shape[1]
    tm = groups * SUBLANES
    slot = lax.rem(i, n_slots)
    next_slot = lax.rem(i + 1, n_slots)
    prev_slot = lax.rem(i + n_slots - 1, n_slots)

    def fetch(tile, s):
        return pltpu.make_async_copy(h_ref.at[pl.ds(tile * groups, groups)], hbuf.at[s], fetch_sem.at[s])

    def await_rows(s):
        for _ in range(TOP_K):
            pltpu.make_async_copy(hbuf.at[s], hbuf.at[s], row_sem.at[s]).wait()

    @pl.when(i == 0)
    def _():
        fetch(0, 0).start()

    @pl.when(i + 1 < pl.num_programs(0))
    def _():
        fetch(i + 1, next_slot).start()

    fetch(i, slot).wait()

    def issue(g, carry):
        for u in range(SUBLANES):
            r = g * SUBLANES + u
            src = hbuf.at[slot, g, pl.ds(u, 1), :]
            pltpu.make_async_copy(src, xs_ref.at[pl.ds(pos_ref[r], 1), :], row_sem.at[slot]).start(priority=0)
            pltpu.make_async_copy(src, xs_ref.at[pl.ds(pos_ref[tm + r], 1), :], row_sem.at[slot]).start(priority=1)
        return carry

    lax.fori_loop(0, groups, issue, 0, unroll=2)

    @pl.when(i >= 1)
    def _():
        await_rows(prev_slot)

    @pl.when(i == pl.num_programs(0) - 1)
    def _():
        await_rows(slot)


def _dispatch(pos_flat, hp, n_rows):
    T, dp = hp.shape
    tm = TM_ROWS
    return pl.pallas_call(
        _dispatch_kernel,
        grid=(T // tm,),
        in_specs=[pl.BlockSpec((TOP_K * tm,), lambda i: (i,), memory_space=pltpu.SMEM),
                  pl.BlockSpec(memory_space=pl.ANY)],
        out_specs=pl.BlockSpec(memory_space=pl.ANY),
        out_shape=jax.ShapeDtypeStruct((n_rows, dp), F32),
        scratch_shapes=[pltpu.VMEM((3, tm // SUBLANES, SUBLANES, dp), F32),
                        pltpu.SemaphoreType.DMA((3,)), pltpu.SemaphoreType.DMA((3,))],
        compiler_params=pltpu.CompilerParams(
            dimension_semantics=("arbitrary",), vmem_limit_bytes=VMEM_LIMIT, has_side_effects=True),
        name="dispatch",
    )(pos_flat, hp.reshape(T // SUBLANES, SUBLANES, dp))


def _experts_kernel(te_ref, tb_ref, tv_ref, xs_ref, wa_ref, wb_ref, wd_ref, ys_ref, xbuf, xb_ref, sem):
    n, f = pl.program_id(0), pl.program_id(1)
    n_tiles = tv_ref.shape[0]
    tm = xb_ref.shape[0]
    slot = lax.rem(n, 2)
    valid = tv_ref[n]

    def fetch(tile, buf_slot):
        rows = pl.ds(pl.multiple_of(tb_ref[tile] * tm, tm), tm)
        return pltpu.make_async_copy(xs_ref.at[rows, :], xbuf.at[buf_slot], sem.at[buf_slot])

    @pl.when(jnp.logical_and(n == 0, f == 0))
    def _():
        fetch(0, 0).start()

    nxt = jnp.minimum(n + 1, n_tiles - 1)

    @pl.when(jnp.logical_and(f == pl.num_programs(1) // 2,
                             jnp.logical_and(n + 1 < n_tiles, tv_ref[nxt] > 0)))
    def _():
        fetch(nxt, 1 - slot).start()

    def hidden_columns_step(first):
        xb = xb_ref[...]
        a = _dot(xb, wa_ref[0].astype(BF16))
        b = _dot(xb, wb_ref[0].astype(BF16))
        act = (a * jax.nn.sigmoid(a) * b).astype(BF16)
        down = _dot(act, wd_ref[0].astype(BF16))
        if first:
            ys_ref[...] = down
        else:
            ys_ref[...] += down

    @pl.when(jnp.logical_and(valid > 0, f == 0))
    def _():
        fetch(n, slot).wait()
        keep = lax.broadcasted_iota(jnp.int32, xb_ref.shape, 0) < valid
        xb_ref[...] = jnp.where(keep, xbuf[slot], 0.0).astype(BF16)
        hidden_columns_step(first=True)

    @pl.when(jnp.logical_and(valid > 0, f > 0))
    def _():
        hidden_columns_step(first=False)


def _experts(tile_e, tile_blk, tile_valid, xs, w_up, w_down):
    n_rows, dp = xs.shape
    tm, tf = TM_MOE, TF_MOE
    nf = D_FF_EXPERT // tf
    nt = tile_e.shape[0]

    def fcol(n, f, tv):
        return jnp.where(tv[n] > 0, f, nf - 1)

    grid_spec = pltpu.PrefetchScalarGridSpec(
        num_scalar_prefetch=3,
        grid=(nt, nf),
        in_specs=[pl.BlockSpec(memory_space=pl.ANY),
                  pl.BlockSpec((1, D_MODEL, tf), lambda n, f, te, tb, tv: (te[n], 0, fcol(n, f, tv))),
                  pl.BlockSpec((1, D_MODEL, tf), lambda n, f, te, tb, tv: (te[n], 0, fcol(n, f, tv) + nf)),
                  pl.BlockSpec((1, tf, D_MODEL), lambda n, f, te, tb, tv: (te[n], fcol(n, f, tv), 0))],
        out_specs=pl.BlockSpec((tm, dp), lambda n, f, te, tb, tv: (tb[n], 0)),
        scratch_shapes=[pltpu.VMEM((2, tm, D_MODEL), F32), pltpu.VMEM((tm, D_MODEL), BF16),
                        pltpu.SemaphoreType.DMA((2,))],
    )
    return pl.pallas_call(
        _experts_kernel,
        grid_spec=grid_spec,
        out_shape=jax.ShapeDtypeStruct((n_rows, dp), F32),
        compiler_params=pltpu.CompilerParams(
            dimension_semantics=("arbitrary", "arbitrary"), vmem_limit_bytes=VMEM_LIMIT),
        name="experts",
    )(tile_e, tile_blk, tile_valid, xs, w_up, w_up, w_down)


def _combine_kernel(pos_ref, next_pos_ref, wts_ref, x_ref, gf_ref, ys_ref, o_ref, y1_ref, y2_ref, sem):
    i = pl.program_id(0)
    tm, d = x_ref.shape
    groups = tm // SUBLANES
    slot = lax.rem(i, 2)

    def gather_tile(rows_ref, buf_slot):
        def issue(g, carry):
            for u in range(SUBLANES):
                r = g * SUBLANES + u
                pltpu.make_async_copy(ys_ref.at[pl.ds(rows_ref[r], 1), :],
                                      y1_ref.at[buf_slot, g, pl.ds(u, 1), :], sem.at[buf_slot]).start(priority=0)
                pltpu.make_async_copy(ys_ref.at[pl.ds(rows_ref[tm + r], 1), :],
                                      y2_ref.at[buf_slot, g, pl.ds(u, 1), :], sem.at[buf_slot]).start(priority=1)
            return carry

        lax.fori_loop(0, groups, issue, 0, unroll=2)

    @pl.when(i == 0)
    def _():
        gather_tile(pos_ref, 0)

    @pl.when(i + 1 < pl.num_programs(0))
    def _():
        gather_tile(next_pos_ref, 1 - slot)

    for _ in range(TOP_K):
        pltpu.make_async_copy(y1_ref.at[slot], y1_ref.at[slot], sem.at[slot]).wait()

    y1 = y1_ref[slot].reshape(tm, d)
    y2 = y2_ref[slot].reshape(tm, d)
    moe = wts_ref[:, 0:1] * y1 + wts_ref[:, 1:2] * y2
    o_ref[...] = _rmsnorm(x_ref[...] + moe, gf_ref[...])


def _combine(pos_flat, wts, x2, g_final, ys):
    T, D = x2.shape
    tm = TM_ROWS
    last = T // tm - 1
    rows_of = lambda index: pl.BlockSpec((TOP_K * tm,), index, memory_space=pltpu.SMEM)
    return pl.pallas_call(
        _combine_kernel,
        grid=(T // tm,),
        in_specs=[rows_of(lambda i: (i,)), rows_of(lambda i: (jnp.minimum(i + 1, last),)),
                  pl.BlockSpec((tm, LANES), lambda i: (i, 0)),
                  pl.BlockSpec((tm, D), lambda i: (i, 0)),
                  _const_spec(g_final.shape),
                  pl.BlockSpec(memory_space=pl.ANY)],
        out_specs=pl.BlockSpec((tm, D), lambda i: (i, 0)),
        out_shape=jax.ShapeDtypeStruct((T, D), F32),
        scratch_shapes=[pltpu.VMEM((2, tm // SUBLANES, SUBLANES, D), F32),
                        pltpu.VMEM((2, tm // SUBLANES, SUBLANES, D), F32), pltpu.SemaphoreType.DMA((2,))],
        compiler_params=pltpu.CompilerParams(
            dimension_semantics=("arbitrary",), vmem_limit_bytes=VMEM_LIMIT),
        name="combine",
    )(pos_flat, pos_flat, wts, x2, g_final, ys)


def _tile_plan(counts, cap, tm, nt):
    tiles_e = (counts + tm - 1) // tm
    ends = jnp.cumsum(tiles_e)
    total = ends[-1]
    n = jnp.minimum(jnp.arange(nt, dtype=jnp.int32), total - 1)
    e = jnp.minimum(jnp.sum(n[:, None] >= ends[None, :], axis=1).astype(jnp.int32), N_EXPERTS - 1)
    local = n - (ends[e] - tiles_e[e])
    blk = e * (cap // tm) + local
    valid = jnp.where(jnp.arange(nt) < total, jnp.clip(counts[e] - local * tm, 0, tm), 0)
    return e, blk.astype(jnp.int32), valid.astype(jnp.int32)


def _moe_layer(x2, g_ffn, w_router, w_up, w_down, g_final):
    T, D = x2.shape
    cap = T
    wr_pad = jnp.pad(w_router, ((0, 0), (0, LANES - N_EXPERTS)))
    hp, wts, pos, cnt = _router(x2, g_ffn, wr_pad, cap)
    pos_flat = pos[:, :TOP_K].reshape(T // TM_ROWS, TM_ROWS, TOP_K).transpose(0, 2, 1).reshape(-1)
    xs = _dispatch(pos_flat, hp, N_EXPERTS * cap)
    nt = TOP_K * T // TM_MOE + N_EXPERTS
    tile_e, tile_blk, tile_valid = _tile_plan(cnt[0, :N_EXPERTS], cap, TM_MOE, nt)
    ys = _experts(tile_e, tile_blk, tile_valid, xs, w_up, w_down)
    return _combine(pos_flat, wts, x2, g_final, ys)


def kernel(x, norm_mix, w_in, b_in, conv_w, conv_b, conv_ln_g, conv_ln_b, conv_out, pool_mix, pool_scale,
           pool_out, sgu_ln_g, sgu_ln_b, sgu_w, sgu_b, sgu_out, w_gate, b_gate, w_o, norm_ffn, ffn_w_up,
           ffn_w_down, moe_router, moe_w_up, moe_w_down, norm_final):
    B, S, D = x.shape
    row = lambda a: a.reshape(1, -1)
    c_conv = 2 * CONV_WIDTH

    for i in range(DEPTH):
        params = {
            "norm": row(norm_mix[i]),
            "w_conv_in": w_in[i, :, :c_conv].astype(BF16), "b_conv_in": row(b_in[i, :c_conv]),
            "w_zg": jnp.concatenate([w_in[i, :, c_conv:], w_gate[i]], axis=1).astype(BF16),
            "b_zg": row(jnp.concatenate([b_in[i, c_conv:], b_gate[i]])),
            "conv_w": jnp.broadcast_to(conv_w[i][:, None, :], (CONV_KERNEL, SUBLANES, CONV_WIDTH)),
            "conv_b": row(conv_b[i]),
            "conv_ln_g": row(conv_ln_g[i]), "conv_ln_b": row(conv_ln_b[i]),
            "conv_out": conv_out[i].astype(BF16),
            "pool_mix": pool_mix[i].astype(BF16), "pool_scale": row(pool_scale[i]),
            "pool_out": pool_out[i].astype(BF16),
            "sgu_ln_g": row(sgu_ln_g[i]), "sgu_ln_b": row(sgu_ln_b[i]), "sgu_w": sgu_w[i],
            "sgu_b": jnp.repeat(sgu_b[i].T, SGU_HEAD_DIM, axis=1),
            "sgu_out": sgu_out[i].astype(BF16),
            "w_o": w_o[i].astype(BF16),
        }
        x = _mixer_layer(x, params)
        x2 = x.reshape(B * S, D)
        if i % 2 == 0:
            x2 = _dense_ffn(x2, row(norm_ffn[i]), ffn_w_up[i // 2].astype(BF16),
                            ffn_w_down[i // 2].astype(BF16))
        else:
            assert i == DEPTH - 1
            x2 = _moe_layer(x2, row(norm_ffn[i]), moe_router[i // 2], moe_w_up[i // 2], moe_w_down[i // 2],
                            row(norm_final))
        x = x2.reshape(B, S, D)
    return x
```

```python
import functools

import jax
import jax.numpy as jnp
from jax import lax
from jax.experimental import pallas as pl
from jax.experimental.pallas import tpu as pltpu

D_MODEL = 1024
DEPTH = 2
CONV_WIDTH = 512
CONV_KERNEL = 31
POOL_WINDOWS = (2, 4, 8, 16)
POOL_GROUP = 128
POOL_WIDTH = POOL_GROUP * len(POOL_WINDOWS)
SGU_HEADS = 4
SGU_HEAD_DIM = 128
SGU_WIDTH = SGU_HEADS * SGU_HEAD_DIM
SGU_BLOCK = 128
N_BRANCH = 3
IN_WIDTH = 2 * CONV_WIDTH + POOL_WIDTH + 2 * SGU_WIDTH
D_FF = 2816
N_EXPERTS = 8
TOP_K = 2
D_FF_EXPERT = 3584
EPS = 1e-6

LANES = 128
SUBLANES = 8
MXU_COLS = 256
N_Z_PIECES = (POOL_WIDTH + 2 * SGU_WIDTH) // MXU_COLS
CONV_HALO = 32
POOL_HALO = 16
VMEM_LIMIT = 56 * 1024 * 1024

TM_MIX = 512
CONV_ROWS = 32
TM_FFN = 512
TM_ROUTE = 512
TM_MOE = 1024
TF_MOE = 512

F32 = jnp.float32
BF16 = jnp.bfloat16


SINGLE_BUFFER_MIN_ELEMS = 256 * 1024


def _const_spec(shape):
    zeros = (0,) * len(shape)
    n_elems = 1
    for d in shape:
        n_elems *= d
    if n_elems < SINGLE_BUFFER_MIN_ELEMS:
        return pl.BlockSpec(shape, lambda *_: zeros)
    return pl.BlockSpec(shape, lambda *_: zeros, pipeline_mode=pl.Buffered(1))


def _rmsnorm(x, g):
    return x * lax.rsqrt(jnp.mean(x * x, axis=-1, keepdims=True) + EPS) * g


def _layernorm(x, g, b):
    mu = jnp.mean(x, axis=-1, keepdims=True)
    xc = x - mu
    var = jnp.mean(xc * xc, axis=-1, keepdims=True)
    return xc * lax.rsqrt(var + EPS) * g + b


def _dot(a, b):
    return jnp.dot(a, b, preferred_element_type=F32)


def _dot_split(a, b):
    a_hi = a.astype(BF16)
    a_lo = (a - a_hi.astype(F32)).astype(BF16)
    b_hi = b.astype(BF16)
    b_lo = (b - b_hi.astype(F32)).astype(BF16)
    return _dot(a_hi, b_hi) + (_dot(a_hi, b_lo) + _dot(a_lo, b_hi))


def _mixer_kernel(x_ref, g_ref, wc_ref, bc_ref, wzg_ref, bzg_ref, cw_ref, cb_ref, clg_ref, clb_ref, cout_ref,
                  pmix_ref, pscale_ref, pout_ref, slg_ref, slb_ref, sw_ref, sb_ref, sout_ref, wo_ref,
                  o_ref, hb_buf, conv_ext, conv_sh, pool_ext, zg_buf, act_buf, m_buf):
    tm = x_ref.shape[1]
    j = pl.program_id(1)

    @pl.when(j == 0)
    def _():
        conv_ext[0:CONV_HALO, :] = jnp.zeros((CONV_HALO, CONV_WIDTH), F32)
        pool_ext[0:POOL_HALO, :] = jnp.zeros((POOL_HALO, POOL_WIDTH), F32)

    @pl.when(j > 0)
    def _():
        conv_ext[0:CONV_HALO, :] = conv_ext[tm:tm + CONV_HALO, :]
        pool_ext[0:POOL_HALO, :] = pool_ext[tm:tm + POOL_HALO, :]

    x = x_ref[0]
    hb_buf[...] = _rmsnorm(x, g_ref[...]).astype(BF16)

    def proj(lo, hi):
        return _dot(hb_buf[...], wc_ref[:, lo:hi]) + bc_ref[:, lo:hi]

    conv_ext[CONV_HALO:, :] = proj(0, CONV_WIDTH) * jax.nn.sigmoid(proj(CONV_WIDTH, 2 * CONV_WIDTH))

    n_groups = conv_sh.shape[1]
    for m in range(SUBLANES):
        groups = n_groups - (1 if m else 0)
        conv_sh[m, 0:groups] = conv_ext[m:m + groups * SUBLANES, :].reshape(groups, SUBLANES, CONV_WIDTH)

    base = CONV_HALO - (CONV_KERNEL - 1)
    gpc = CONV_ROWS // SUBLANES

    def conv_chunk(ci):
        acc = jnp.zeros((gpc, SUBLANES, CONV_WIDTH), F32) + cb_ref[...]
        for k in range(CONV_KERNEL):
            q, m = divmod(base + k, SUBLANES)
            acc = acc + conv_sh[m, pl.ds(ci * gpc + q, gpc)] * cw_ref[k]
        hn = _layernorm(acc.reshape(CONV_ROWS, CONV_WIDTH), clg_ref[...], clb_ref[...])
        rows = pl.ds(pl.multiple_of(ci * CONV_ROWS, CONV_ROWS), CONV_ROWS)
        act_buf[rows, :] = (hn * jax.nn.sigmoid(hn)).astype(BF16)

    def piece(c, act):
        cols = slice(c * MXU_COLS, (c + 1) * MXU_COLS)
        zg_buf[c] = act(_dot(hb_buf[...], wzg_ref[:, cols]) + bzg_ref[:, cols])

    n_chunks = tm // CONV_ROWS
    n_pieces = zg_buf.shape[0]
    assert n_chunks <= n_pieces
    for c in range(n_pieces):
        piece(c, (lambda z: z) if c < N_Z_PIECES else jax.nn.sigmoid)
        if c >= n_pieces - n_chunks:
            conv_chunk(c - (n_pieces - n_chunks))

    def z_cols(lo, hi):
        return jnp.concatenate([zg_buf[c] for c in range(lo // MXU_COLS, hi // MXU_COLS)], axis=-1)

    def gate(k):
        c0 = N_Z_PIECES + k * (D_MODEL // MXU_COLS)
        return jnp.concatenate([zg_buf[c] for c in range(c0, c0 + D_MODEL // MXU_COLS)], axis=-1)

    p = z_cols(0, POOL_WIDTH)
    pool_ext[POOL_HALO:, :] = p
    head_pos = lax.broadcasted_iota(jnp.int32, (POOL_HALO, POOL_GROUP), 0)
    diffs = []
    for gi, w in enumerate(POOL_WINDOWS):
        lo, hi = gi * POOL_GROUP, (gi + 1) * POOL_GROUP
        s = pool_ext[:, lo:hi]
        step = 1
        while step < w:
            s = s + pltpu.roll(s, step, axis=0)
            step *= 2
        s = s[POOL_HALO:, :]
        mean = s * (1.0 / w)
        head_cnt = jnp.minimum(head_pos + 1, w).astype(F32)
        head = jnp.where(j == 0, s[:POOL_HALO, :] / head_cnt, mean[:POOL_HALO, :])
        mean = jnp.concatenate([head, mean[POOL_HALO:, :]], axis=0)
        diffs.append((mean - p[:, lo:hi]).astype(BF16))
    u = jax.nn.gelu(z_cols(POOL_WIDTH, POOL_WIDTH + SGU_WIDTH), approximate=True)
    mixed = [_dot(diffs[gi], pmix_ref[gi]) for gi in range(len(POOL_WINDOWS))]
    yb = jnp.concatenate(mixed, axis=-1) * pscale_ref[...]
    v = jax.nn.gelu(z_cols(POOL_WIDTH + SGU_WIDTH, POOL_WIDTH + 2 * SGU_WIDTH), approximate=True)
    vn = _layernorm(v, slg_ref[...], slb_ref[...]).astype(BF16)
    m_buf[...] = gate(1) * _dot(yb.astype(BF16), pout_ref[...])

    row = lax.broadcasted_iota(jnp.int32, (SGU_BLOCK, SGU_BLOCK), 0)
    col = lax.broadcasted_iota(jnp.int32, (SGU_BLOCK, SGU_BLOCK), 1)
    ws = [jnp.where(row >= col, sw_ref[hd], 0.0).astype(BF16) for hd in range(SGU_HEADS)]
    blocks = []
    for r in range(0, tm, SGU_BLOCK):
        heads = [_dot(ws[hd], vn[r:r + SGU_BLOCK, hd * SGU_HEAD_DIM:(hd + 1) * SGU_HEAD_DIM])
                 for hd in range(SGU_HEADS)]
        blocks.append(jnp.concatenate(heads, axis=-1) + sb_ref[...])
    y = u * jnp.concatenate(blocks, axis=0)
    m_buf[...] += gate(2) * _dot(y.astype(BF16), sout_ref[...])
    merged = m_buf[...] + gate(0) * _dot(act_buf[...], cout_ref[...])

    o_ref[0] = x + _dot(merged.astype(BF16), wo_ref[...])


def _mixer_layer(x, p):
    B, S, D = x.shape
    tm = TM_MIX
    consts = [p["norm"], p["w_conv_in"], p["b_conv_in"], p["w_zg"], p["b_zg"], p["conv_w"], p["conv_b"],
              p["conv_ln_g"], p["conv_ln_b"], p["conv_out"], p["pool_mix"], p["pool_scale"], p["pool_out"],
              p["sgu_ln_g"], p["sgu_ln_b"], p["sgu_w"], p["sgu_b"], p["sgu_out"], p["w_o"]]
    x_spec = pl.BlockSpec((1, tm, D), lambda b, j: (b, j, 0))
    n_pieces = p["w_zg"].shape[1] // MXU_COLS
    return pl.pallas_call(
        _mixer_kernel,
        grid=(B, S // tm),
        in_specs=[x_spec] + [_const_spec(c.shape) for c in consts],
        out_specs=x_spec,
        out_shape=jax.ShapeDtypeStruct(x.shape, F32),
        scratch_shapes=[pltpu.VMEM((tm, D), BF16),
                        pltpu.VMEM((tm + CONV_HALO, CONV_WIDTH), F32),
                        pltpu.VMEM((SUBLANES, (tm + CONV_HALO) // SUBLANES, SUBLANES, CONV_WIDTH), F32),
                        pltpu.VMEM((tm + POOL_HALO, POOL_WIDTH), F32),
                        pltpu.VMEM((n_pieces, tm, MXU_COLS), F32),
                        pltpu.VMEM((tm, CONV_WIDTH), BF16),
                        pltpu.VMEM((tm, D), F32)],
        compiler_params=pltpu.CompilerParams(
            dimension_semantics=("arbitrary", "arbitrary"), vmem_limit_bytes=VMEM_LIMIT),
        name="mixer",
    )(x, *consts)


def _ffn_kernel(x_ref, g_ref, wup_ref, wdn_ref, o_ref):
    x = x_ref[...]
    hb = _rmsnorm(x, g_ref[...]).astype(BF16)
    a = _dot(hb, wup_ref[:, :D_FF])
    b = _dot(hb, wup_ref[:, D_FF:])
    act = (a * jax.nn.sigmoid(a) * b).astype(BF16)
    o_ref[...] = x + _dot(act, wdn_ref[...])


def _dense_ffn(x2, g, w_up, w_down):
    T, D = x2.shape
    tm = TM_FFN
    x_spec = pl.BlockSpec((tm, D), lambda i: (i, 0))
    return pl.pallas_call(
        _ffn_kernel,
        grid=(T // tm,),
        in_specs=[x_spec, _const_spec(g.shape), _const_spec(w_up.shape), _const_spec(w_down.shape)],
        out_specs=x_spec,
        out_shape=jax.ShapeDtypeStruct(x2.shape, F32),
        compiler_params=pltpu.CompilerParams(
            dimension_semantics=("arbitrary",), vmem_limit_bytes=VMEM_LIMIT),
        name="dense_ffn",
    )(x2, g, w_up, w_down)


def _router_kernel(x_ref, g_ref, wr_ref, hp_ref, wts_ref, pos_ref, cnt_ref, carry_ref, *, cap):
    tm = x_ref.shape[0]

    @pl.when(pl.program_id(0) == 0)
    def _():
        carry_ref[...] = jnp.zeros_like(carry_ref)

    h = _rmsnorm(x_ref[...], g_ref[...])
    hp_ref[...] = h
    logits = _dot_split(h, wr_ref[...])
    lane = lax.broadcasted_iota(jnp.int32, logits.shape, 1)
    neg = jnp.float32(-jnp.inf)
    logits = jnp.where(lane < N_EXPERTS, logits, neg)
    m1 = jnp.max(logits, axis=-1, keepdims=True)
    i1 = jnp.min(jnp.where(logits == m1, lane, LANES), axis=-1, keepdims=True)
    rest = jnp.where(lane == i1, neg, logits)
    m2 = jnp.max(rest, axis=-1, keepdims=True)
    i2 = jnp.min(jnp.where(rest == m2, lane, LANES), axis=-1, keepdims=True)
    e2 = jnp.exp(m2 - m1)
    w1 = 1.0 / (1.0 + e2)
    w2 = e2 / (1.0 + e2)

    chosen = jnp.where(jnp.logical_or(lane == i1, lane == i2), 1.0, 0.0)
    r_io = lax.broadcasted_iota(jnp.int32, (tm, tm), 0)
    c_io = lax.broadcasted_iota(jnp.int32, (tm, tm), 1)
    tri = jnp.where(r_io >= c_io, 1.0, 0.0).astype(BF16)
    incl = _dot(tri, chosen.astype(BF16))
    before = incl - chosen + carry_ref[...]
    rank1 = jnp.sum(jnp.where(lane == i1, before, 0.0), axis=-1, keepdims=True).astype(jnp.int32)
    rank2 = jnp.sum(jnp.where(lane == i2, before, 0.0), axis=-1, keepdims=True).astype(jnp.int32)
    carry_ref[...] = carry_ref[...] + incl[tm - 1:tm, :]
    cnt_ref[...] = carry_ref[...].astype(jnp.int32)

    wts_ref[...] = jnp.where(lane == 0, w1, jnp.where(lane == 1, w2, 0.0))
    pos_ref[...] = jnp.where(lane == 0, i1 * cap + rank1, jnp.where(lane == 1, i2 * cap + rank2, 0))


def _router(x2, g, wr_pad, cap):
    T, D = x2.shape
    tm = TM_ROUTE
    tile = lambda w: pl.BlockSpec((tm, w), lambda i: (i, 0))
    return pl.pallas_call(
        functools.partial(_router_kernel, cap=cap),
        grid=(T // tm,),
        in_specs=[tile(D), _const_spec(g.shape), _const_spec(wr_pad.shape)],
        out_specs=[tile(D), tile(LANES), tile(LANES), pl.BlockSpec((1, LANES), lambda i: (0, 0))],
        out_shape=[jax.ShapeDtypeStruct((T, D), F32), jax.ShapeDtypeStruct((T, LANES), F32),
                   jax.ShapeDtypeStruct((T, LANES), jnp.int32), jax.ShapeDtypeStruct((1, LANES), jnp.int32)],
        scratch_shapes=[pltpu.VMEM((1, LANES), F32)],
        compiler_params=pltpu.CompilerParams(
            dimension_semantics=("arbitrary",), vmem_limit_bytes=VMEM_LIMIT),
        name="router",
    )(x2, g, wr_pad)


def _dispatch_kernel(pos_ref, h_ref, xs_ref, hbuf, fetch_sem, row_sem):
    i = pl.program_id(0)
    n_slots, groups = hbuf.shape[0], hbuf.shape[1]
    tm = groups * SUBLANES
    slot = lax.rem(i, n_slots)
    next_slot = lax.rem(i + 1, n_slots)
    prev_slot = lax.rem(i + n_slots - 1, n_slots)

    def fetch(tile, s):
        return pltpu.make_async_copy(h_ref.at[pl.ds(tile * groups, groups)], hbuf.at[s], fetch_sem.at[s])

    def await_rows(s):
        for _ in range(TOP_K):
            pltpu.make_async_copy(hbuf.at[s], hbuf.at[s], row_sem.at[s]).wait()

    @pl.when(i == 0)
    def _():
        fetch(0, 0).start()

    @pl.when(i + 1 < pl.num_programs(0))
    def _():
        fetch(i + 1, next_slot).start()

    fetch(i, slot).wait()

    def issue(g, carry):
        for u in range(SUBLANES):
            r = g * SUBLANES + u
            src = hbuf.at[slot, g, pl.ds(u, 1), :]
            pltpu.make_async_copy(src, xs_ref.at[pl.ds(pos_ref[r], 1), :], row_sem.at[slot]).start(priority=0)
            pltpu.make_async_copy(src, xs_ref.at[pl.ds(pos_ref[tm + r], 1), :], row_sem.at[slot]).start(priority=1)
        return carry

    lax.fori_loop(0, groups, issue, 0)

    @pl.when(i >= 1)
    def _():
        await_rows(prev_slot)

    @pl.when(i == pl.num_programs(0) - 1)
    def _():
        await_rows(slot)


def _dispatch(pos_flat, hp, n_rows):
    T, dp = hp.shape
    tm = TM_ROUTE
    return pl.pallas_call(
        _dispatch_kernel,
        grid=(T // tm,),
        in_specs=[pl.BlockSpec((TOP_K * tm,), lambda i: (i,), memory_space=pltpu.SMEM),
                  pl.BlockSpec(memory_space=pl.ANY)],
        out_specs=pl.BlockSpec(memory_space=pl.ANY),
        out_shape=jax.ShapeDtypeStruct((n_rows, dp), F32),
        scratch_shapes=[pltpu.VMEM((3, tm // SUBLANES, SUBLANES, dp), F32),
                        pltpu.SemaphoreType.DMA((3,)), pltpu.SemaphoreType.DMA((3,))],
        compiler_params=pltpu.CompilerParams(
            dimension_semantics=("arbitrary",), vmem_limit_bytes=VMEM_LIMIT, has_side_effects=True),
        name="dispatch",
    )(pos_flat, hp.reshape(T // SUBLANES, SUBLANES, dp))


def _experts_kernel(te_ref, tb_ref, tv_ref, xs_ref, wa_ref, wb_ref, wd_ref, ys_ref, xbuf, xb_ref, sem):
    n, f = pl.program_id(0), pl.program_id(1)
    n_tiles = tv_ref.shape[0]
    tm = xb_ref.shape[0]
    slot = lax.rem(n, 2)
    valid = tv_ref[n]

    def fetch(tile, buf_slot):
        rows = pl.ds(pl.multiple_of(tb_ref[tile] * tm, tm), tm)
        return pltpu.make_async_copy(xs_ref.at[rows, :], xbuf.at[buf_slot], sem.at[buf_slot])

    @pl.when(jnp.logical_and(n == 0, f == 0))
    def _():
        fetch(0, 0).start()

    nxt = jnp.minimum(n + 1, n_tiles - 1)

    @pl.when(jnp.logical_and(f == pl.num_programs(1) // 2,
                             jnp.logical_and(n + 1 < n_tiles, tv_ref[nxt] > 0)))
    def _():
        fetch(nxt, 1 - slot).start()

    def hidden_columns_step(first):
        xb = xb_ref[...]
        a = _dot(xb, wa_ref[0].astype(BF16))
        b = _dot(xb, wb_ref[0].astype(BF16))
        act = (a * jax.nn.sigmoid(a) * b).astype(BF16)
        down = _dot(act, wd_ref[0].astype(BF16))
        if first:
            ys_ref[...] = down
        else:
            ys_ref[...] += down

    @pl.when(jnp.logical_and(valid > 0, f == 0))
    def _():
        fetch(n, slot).wait()
        keep = lax.broadcasted_iota(jnp.int32, xb_ref.shape, 0) < valid
        xb_ref[...] = jnp.where(keep, xbuf[slot], 0.0).astype(BF16)
        hidden_columns_step(first=True)

    @pl.when(jnp.logical_and(valid > 0, f > 0))
    def _():
        hidden_columns_step(first=False)


def _experts(tile_e, tile_blk, tile_valid, xs, w_up, w_down):
    n_rows, dp = xs.shape
    tm, tf = TM_MOE, TF_MOE
    nf = D_FF_EXPERT // tf
    nt = tile_e.shape[0]

    def fcol(n, f, tv):
        return jnp.where(tv[n] > 0, f, nf - 1)

    grid_spec = pltpu.PrefetchScalarGridSpec(
        num_scalar_prefetch=3,
        grid=(nt, nf),
        in_specs=[pl.BlockSpec(memory_space=pl.ANY),
                  pl.BlockSpec((1, D_MODEL, tf), lambda n, f, te, tb, tv: (te[n], 0, fcol(n, f, tv))),
                  pl.BlockSpec((1, D_MODEL, tf), lambda n, f, te, tb, tv: (te[n], 0, fcol(n, f, tv) + nf)),
                  pl.BlockSpec((1, tf, D_MODEL), lambda n, f, te, tb, tv: (te[n], fcol(n, f, tv), 0))],
        out_specs=pl.BlockSpec((tm, dp), lambda n, f, te, tb, tv: (tb[n], 0)),
        scratch_shapes=[pltpu.VMEM((2, tm, D_MODEL), F32), pltpu.VMEM((tm, D_MODEL), BF16),
                        pltpu.SemaphoreType.DMA((2,))],
    )
    return pl.pallas_call(
        _experts_kernel,
        grid_spec=grid_spec,
        out_shape=jax.ShapeDtypeStruct((n_rows, dp), F32),
        compiler_params=pltpu.CompilerParams(
            dimension_semantics=("arbitrary", "arbitrary"), vmem_limit_bytes=VMEM_LIMIT),
        name="experts",
    )(tile_e, tile_blk, tile_valid, xs, w_up, w_up, w_down)


def _combine_kernel(pos_ref, next_pos_ref, wts_ref, x_ref, gf_ref, ys_ref, o_ref, y1_ref, y2_ref, sem):
    i = pl.program_id(0)
    tm, d = x_ref.shape
    groups = tm // SUBLANES
    slot = lax.rem(i, 2)

    def gather_tile(rows_ref, buf_slot):
        def issue(g, carry):
            for u in range(SUBLANES):
                r = g * SUBLANES + u
                pltpu.make_async_copy(ys_ref.at[pl.ds(rows_ref[r], 1), :],
                                      y1_ref.at[buf_slot, g, pl.ds(u, 1), :], sem.at[buf_slot]).start(priority=0)
                pltpu.make_async_copy(ys_ref.at[pl.ds(rows_ref[tm + r], 1), :],
                                      y2_ref.at[buf_slot, g, pl.ds(u, 1), :], sem.at[buf_slot]).start(priority=1)
            return carry

        lax.fori_loop(0, groups, issue, 0)

    @pl.when(i == 0)
    def _():
        gather_tile(pos_ref, 0)

    @pl.when(i + 1 < pl.num_programs(0))
    def _():
        gather_tile(next_pos_ref, 1 - slot)

    for _ in range(TOP_K):
        pltpu.make_async_copy(y1_ref.at[slot], y1_ref.at[slot], sem.at[slot]).wait()

    y1 = y1_ref[slot].reshape(tm, d)
    y2 = y2_ref[slot].reshape(tm, d)
    moe = wts_ref[:, 0:1] * y1 + wts_ref[:, 1:2] * y2
    o_ref[...] = _rmsnorm(x_ref[...] + moe, gf_ref[...])


def _combine(pos_flat, wts, x2, g_final, ys):
    T, D = x2.shape
    tm = TM_ROUTE
    last = T // tm - 1
    rows_of = lambda index: pl.BlockSpec((TOP_K * tm,), index, memory_space=pltpu.SMEM)
    return pl.pallas_call(
        _combine_kernel,
        grid=(T // tm,),
        in_specs=[rows_of(lambda i: (i,)), rows_of(lambda i: (jnp.minimum(i + 1, last),)),
                  pl.BlockSpec((tm, LANES), lambda i: (i, 0)),
                  pl.BlockSpec((tm, D), lambda i: (i, 0)),
                  _const_spec(g_final.shape),
                  pl.BlockSpec(memory_space=pl.ANY)],
        out_specs=pl.BlockSpec((tm, D), lambda i: (i, 0)),
        out_shape=jax.ShapeDtypeStruct((T, D), F32),
        scratch_shapes=[pltpu.VMEM((2, tm // SUBLANES, SUBLANES, D), F32),
                        pltpu.VMEM((2, tm // SUBLANES, SUBLANES, D), F32), pltpu.SemaphoreType.DMA((2,))],
        compiler_params=pltpu.CompilerParams(
            dimension_semantics=("arbitrary",), vmem_limit_bytes=VMEM_LIMIT),
        name="combine",
    )(pos_flat, pos_flat, wts, x2, g_final, ys)


def _tile_plan(counts, cap, tm, nt):
    tiles_e = (counts + tm - 1) // tm
    ends = jnp.cumsum(tiles_e)
    total = ends[-1]
    n = jnp.minimum(jnp.arange(nt, dtype=jnp.int32), total - 1)
    e = jnp.minimum(jnp.sum(n[:, None] >= ends[None, :], axis=1).astype(jnp.int32), N_EXPERTS - 1)
    local = n - (ends[e] - tiles_e[e])
    blk = e * (cap // tm) + local
    valid = jnp.where(jnp.arange(nt) < total, jnp.clip(counts[e] - local * tm, 0, tm), 0)
    return e, blk.astype(jnp.int32), valid.astype(jnp.int32)


def _moe_layer(x2, g_ffn, w_router, w_up, w_down, g_final):
    T, D = x2.shape
    cap = T
    wr_pad = jnp.pad(w_router, ((0, 0), (0, LANES - N_EXPERTS)))
    hp, wts, pos, cnt = _router(x2, g_ffn, wr_pad, cap)
    pos_flat = pos[:, :TOP_K].reshape(T // TM_ROUTE, TM_ROUTE, TOP_K).transpose(0, 2, 1).reshape(-1)
    xs = _dispatch(pos_flat, hp, N_EXPERTS * cap)
    nt = TOP_K * T // TM_MOE + N_EXPERTS
    tile_e, tile_blk, tile_valid = _tile_plan(cnt[0, :N_EXPERTS], cap, TM_MOE, nt)
    ys = _experts(tile_e, tile_blk, tile_valid, xs, w_up, w_down)
    return _combine(pos_flat, wts, x2, g_final, ys)


def kernel(x, norm_mix, w_in, b_in, conv_w, conv_b, conv_ln_g, conv_ln_b, conv_out, pool_mix, pool_scale,
           pool_out, sgu_ln_g, sgu_ln_b, sgu_w, sgu_b, sgu_out, w_gate, b_gate, w_o, norm_ffn, ffn_w_up,
           ffn_w_down, moe_router, moe_w_up, moe_w_down, norm_final):
    B, S, D = x.shape
    row = lambda a: a.reshape(1, -1)
    c_conv = 2 * CONV_WIDTH

    for i in range(DEPTH):
        params = {
            "norm": row(norm_mix[i]),
            "w_conv_in": w_in[i, :, :c_conv].astype(BF16), "b_conv_in": row(b_in[i, :c_conv]),
            "w_zg": jnp.concatenate([w_in[i, :, c_conv:], w_gate[i]], axis=1).astype(BF16),
            "b_zg": row(jnp.concatenate([b_in[i, c_conv:], b_gate[i]])),
            "conv_w": jnp.broadcast_to(conv_w[i][:, None, :], (CONV_KERNEL, SUBLANES, CONV_WIDTH)),
            "conv_b": row(conv_b[i]),
            "conv_ln_g": row(conv_ln_g[i]), "conv_ln_b": row(conv_ln_b[i]),
            "conv_out": conv_out[i].astype(BF16),
            "pool_mix": pool_mix[i].astype(BF16), "pool_scale": row(pool_scale[i]),
            "pool_out": pool_out[i].astype(BF16),
            "sgu_ln_g": row(sgu_ln_g[i]), "sgu_ln_b": row(sgu_ln_b[i]), "sgu_w": sgu_w[i],
            "sgu_b": jnp.repeat(sgu_b[i].T, SGU_HEAD_DIM, axis=1),
            "sgu_out": sgu_out[i].astype(BF16),
            "w_o": w_o[i].astype(BF16),
        }
        x = _mixer_layer(x, params)
        x2 = x.reshape(B * S, D)
        if i % 2 == 0:
            x2 = _dense_ffn(x2, row(norm_ffn[i]), ffn_w_up[i // 2].astype(BF16),
                            ffn_w_down[i // 2].astype(BF16))
        else:
            assert i == DEPTH - 1
            x2 = _moe_layer(x2, row(norm_ffn[i]), moe_router[i // 2], moe_w_up[i // 2], moe_w_down[i // 2],
                            row(norm_final))
        x = x2.reshape(B, S, D)
    return x
```
